```python
import math
import jax, jax.numpy as jnp
from jax import lax
import numpy as np


D_MODEL = 1024
BATCH = 8
SEQ = 4096
DEPTH = 2
DEC_BATCH = 2
DEC_SEQ = 16384
PAST_LEN = 128

D_MIX = D_MODEL
HY_W = D_MIX // 4
HG_W = D_MIX // 2
RG_W = D_MIX - HY_W - HG_W
HY_SHORT = 3
HY_EMB = 33
HY_BANDS = (HY_EMB - 1) // 2
HY_FFN = 64
HY_FAST_DECAY = 0.3
HY_SLOW_DECAY = 1.5
HY_TARGET = 1e-2
HG_HEADS = 4
HG_DK = HG_W // HG_HEADS
HG_DV = HG_W // HG_HEADS
HG_CHUNK = 64
RG_HEADS = 4
RG_HD = RG_W // RG_HEADS
RG_CONV = 4
RG_C = 8.0
D_FF = -(-8 * D_MODEL // (3 * 256)) * 256
HY_IN = 3 * HY_W
HG_IN = 5 * HG_W
RG_IN = 2 * RG_W
D_IN = HY_IN + HG_IN + RG_IN
EPS = 1e-6

kernel_name = 'hymba_style_hyena_hgrn2_rglru_encoder'


def rmsnorm(x, w):
    xf = x.astype(jnp.float32)
    y = xf * lax.rsqrt(jnp.mean(xf * xf, axis=-1, keepdims=True) + EPS)
    return (y * w.astype(jnp.float32)).astype(x.dtype)


def dwconv(x, w, b, left):
    K = w.shape[0]
    L = x.shape[1]
    xp = jnp.pad(x, ((0, 0), (left, K - 1 - left), (0, 0)))
    y = b
    for k in range(K):
        y = y + xp[:, k:k + L] * w[k]
    return y


def hyena_filter(L, w1, b1, w2, b2, w3, b3, w4, freq):
    f32 = jnp.float32
    t = jnp.linspace(0.0, 1.0, L, dtype=f32)[:, None]
    n = jnp.arange(L, dtype=f32)[:, None]
    fb = jnp.linspace(1e-4, HY_BANDS - 1, HY_BANDS, dtype=f32)[None, :]
    ang = fb * n * (2.0 * math.pi / L)
    z = jnp.concatenate([t, jnp.cos(ang), -jnp.sin(ang)], axis=-1)
    fr = freq.astype(f32)
    h = jnp.sin(fr * (z @ w1.astype(f32) + b1.astype(f32)))
    h = jnp.sin(fr * (h @ w2.astype(f32) + b2.astype(f32)))
    h = jnp.sin(fr * (h @ w3.astype(f32) + b3.astype(f32)))
    h = h @ w4.astype(f32)
    deltas = jnp.abs(jnp.linspace(math.log(HY_TARGET) / HY_SLOW_DECAY,
                                  math.log(HY_TARGET) / HY_FAST_DECAY, HY_W, dtype=f32))
    decay = jnp.exp(-t * deltas)
    return h * jnp.concatenate([decay, decay], axis=-1)


def hyena_mixer(u, conv_w, conv_b, w1, b1, w2, b2, w3, b3, w4, freq, bias):
    B, L, _ = u.shape
    uc = dwconv(u, conv_w, conv_b, (HY_SHORT - 1) // 2)
    x0, x1, v = jnp.split(uc, 3, axis=-1)
    v = (v * x1).astype(jnp.float32)
    k = hyena_filter(L, w1, b1, w2, b2, w3, b3, w4, freq)
    kf, kb = k[:, :HY_W], k[:, HY_W:]
    kc = jnp.concatenate([kf, jnp.zeros((1, HY_W), jnp.float32), kb[:0:-1]], axis=0)
    spec = jnp.fft.rfft(v, n=2 * L, axis=1) * jnp.fft.rfft(kc, n=2 * L, axis=0)[None]
    y = jnp.fft.irfft(spec, n=2 * L, axis=1)[:, :L]
    y = y + v * bias.astype(jnp.float32)
    return y * x0.astype(jnp.float32)


def hgrn2_scan(q, k, v, logf):
    B, L, H, Dk = q.shape
    Dv = v.shape[-1]
    C = HG_CHUNK
    n = L // C

    def chunks(a):
        return a.reshape(B, n, C, H, a.shape[-1]).transpose(1, 0, 3, 2, 4)

    causal = jnp.tril(jnp.ones((C, C), bool))[:, :, None]

    def step(S, inp):
        qc, kc, vc, gc = inp
        b = jnp.cumsum(gc, axis=2)
        inter = jnp.einsum('bhtk,bhkv->bhtv', qc * jnp.exp(b), S)
        diff = b[:, :, :, None, :] - b[:, :, None, :, :]
        w = jnp.where(causal, jnp.exp(jnp.where(causal, diff, 0.0)), 0.0)
        A = jnp.einsum('bhtk,bhtsk,bhsk->bhts', qc, w, kc)
        o = inter + jnp.einsum('bhts,bhsv->bhtv', A, vc)
        bl = b[:, :, -1:, :]
        S = jnp.exp(bl[:, :, 0, :])[..., None] * S + jnp.einsum('bhsk,bhsv->bhkv', kc * jnp.exp(bl - b), vc)
        return S, o

    S0 = jnp.zeros((B, H, Dk, Dv), jnp.float32)
    _, o = lax.scan(step, S0, (chunks(q), chunks(k), chunks(v), chunks(logf)))
    return o.transpose(1, 0, 3, 2, 4).reshape(B, L, H, Dv)


def hgrn2_mixer(u, lb_f, lb_b):
    B, L, _ = u.shape
    q, i, zf, zb, g = jnp.split(u, 5, axis=-1)

    def heads(a):
        return a.astype(jnp.float32).reshape(B, L, HG_HEADS, -1)

    q, v = heads(q), heads(i)

    def direction(z, lb, qd, vd):
        z = heads(z)
        lb = lb.astype(jnp.float32).reshape(HG_HEADS, HG_DK)
        logf = jax.nn.log_sigmoid(z) + jnp.log1p(lb * jnp.exp(-z))
        kk = (1.0 - lb) * jax.nn.sigmoid(-z)
        return hgrn2_scan(qd, kk, vd, logf)

    o_f = direction(zf, lb_f, q, v)
    o_b = jnp.flip(direction(jnp.flip(zb, 1), lb_b, jnp.flip(q, 1), jnp.flip(v, 1)), 1)
    return o_f + o_b, g


def lru_combine(e1, e2):
    a1, b1 = e1
    a2, b2 = e2
    return (a1 * a2, a2 * b1 + b2)


def rglru_mixer(u, conv_w, conv_b, wa, ba, wx, bx, lam):
    B, L, _ = u.shape
    xr_in, gate = jnp.split(u, 2, axis=-1)
    xr = dwconv(xr_in, conv_w, conv_b, RG_CONV // 2).astype(jnp.float32)

    def blockdiag(a, w):
        return jnp.einsum('blhi,hij->blhj', a.reshape(B, L, RG_HEADS, RG_HD),
                          w.astype(jnp.float32)).reshape(B, L, RG_W)

    def direction(xd, d):
        r = jax.nn.sigmoid(blockdiag(xd, wa[d]) + ba[d].astype(jnp.float32))
        ig = jax.nn.sigmoid(blockdiag(xd, wx[d]) + bx[d].astype(jnp.float32))
        log_a = -RG_C * r * jax.nn.softplus(-lam[d].astype(jnp.float32))
        a = jnp.exp(log_a)
        bterm = jnp.sqrt(-jnp.expm1(2.0 * log_a)) * (ig * xd)
        _, h = lax.associative_scan(lru_combine, (a, bterm), axis=1)
        return h

    h = direction(xr, 0) + jnp.flip(direction(jnp.flip(xr, 1), 1), 1)
    return h * jax.nn.gelu(gate.astype(jnp.float32))


def token_mixer(h, l, lb_f, lb_b, p):
    B, L, _ = h.shape
    u = h @ p['w_in'][l]
    u_hy = u[..., :HY_IN]
    u_hg = u[..., HY_IN:HY_IN + HG_IN]
    u_rg = u[..., HY_IN + HG_IN:]
    y_hy = hyena_mixer(u_hy, p['hy_conv_w'][l], p['hy_conv_b'][l], p['hy_w1'][l], p['hy_b1'][l],
                       p['hy_w2'][l], p['hy_b2'][l], p['hy_w3'][l], p['hy_b3'][l], p['hy_w4'][l],
                       p['hy_freq'][l], p['hy_bias'][l])
    o_hg, g_hg = hgrn2_mixer(u_hg, lb_f, lb_b)
    y_rg = rglru_mixer(u_rg, p['rg_conv_w'][l], p['rg_conv_b'][l], p['rg_wa'][l], p['rg_ba'][l],
                       p['rg_wx'][l], p['rg_bx'][l], p['rg_lambda'][l])
    gain = p['out_norm_w'][l]
    y_hy = rmsnorm(y_hy, gain[:HY_W])
    y_hg = rmsnorm(o_hg, gain[HY_W:HY_W + HG_W].reshape(HG_HEADS, HG_DV)).reshape(B, L, HG_W)
    y_hg = y_hg * jax.nn.silu(g_hg.astype(jnp.float32))
    y_rg = rmsnorm(y_rg, gain[HY_W + HG_W:])
    y = jnp.concatenate([y_hy, y_hg, y_rg], axis=-1).astype(h.dtype)
    return y @ p['w_out'][l]


def swiglu(h, wg, wu, wd):
    return (jax.nn.silu(h @ wg) * (h @ wu)) @ wd


def trunk(x, c, p):
    lb_all = jax.nn.softmax(p['hg_lb_logits'].astype(jnp.float32), axis=1)
    lb_all = jnp.cumsum(lb_all, axis=1) - lb_all[:, :1]
    for l in range(DEPTH):
        mod = jax.nn.silu(c) @ p['ada_w'][l] + p['ada_b'][l]
        sh1, sc1, g1, sh2, sc2, g2 = jnp.split(mod[:, None, :], 6, axis=-1)
        h = rmsnorm(x, p['norm1_w'][l]) * (1 + sc1) + sh1
        x = x + g1 * token_mixer(h, l, lb_all[0, l], lb_all[1, l], p)
        h = rmsnorm(x, p['norm2_w'][l]) * (1 + sc2) + sh2
        x = x + g2 * swiglu(h, p['ffn_wg'][l], p['ffn_wu'][l], p['ffn_wd'][l])
    return rmsnorm(x, p['final_norm_w'])


def setup_inputs(seed: int = 0) -> dict:
    key = jax.random.key(seed)
    ks = jax.random.split(key, 34)
    f32 = jnp.float32

    def nrm(i, shape, scale):
        return scale * jax.random.normal(ks[i], shape, f32)

    def gain(i, shape):
        return 1.0 + 0.05 * jax.random.normal(ks[i], shape, f32)

    u = jax.random.uniform(ks[30], (DEPTH, 2, RG_W), f32, minval=0.9, maxval=0.999)
    s = u ** (1.0 / RG_C)
    rg_lambda = jnp.log(s) - jnp.log1p(-s)
    return {
        'x_prompt': nrm(0, (BATCH, SEQ, D_MODEL), 1.0),
        'x_sample': nrm(1, (DEC_BATCH, DEC_SEQ, D_MODEL), 1.0),
        'c_prompt': nrm(2, (BATCH, D_MODEL), 1.0),
        'c_sample': nrm(3, (DEC_BATCH, D_MODEL), 1.0),
        'w_in': nrm(4, (DEPTH, D_MODEL, D_IN), D_MODEL ** -0.5),
        'w_out': nrm(5, (DEPTH, D_MIX, D_MODEL), D_MIX ** -0.5),
        'out_norm_w': gain(6, (DEPTH, D_MIX)),
        'ada_w': nrm(7, (DEPTH, D_MODEL, 6 * D_MODEL), 0.5 * D_MODEL ** -0.5),
        'ada_b': nrm(8, (DEPTH, 6 * D_MODEL), 0.02),
        'norm1_w': gain(9, (DEPTH, D_MODEL)),
        'norm2_w': gain(10, (DEPTH, D_MODEL)),
        'final_norm_w': gain(11, (D_MODEL,)),
        'hy_conv_w': nrm(12, (DEPTH, HY_SHORT, HY_IN), HY_SHORT ** -0.5),
        'hy_conv_b': nrm(13, (DEPTH, HY_IN), 0.02),
        'hy_w1': nrm(14, (DEPTH, HY_EMB, HY_FFN), HY_EMB ** -0.5),
        'hy_b1': nrm(15, (DEPTH, HY_FFN), 0.1),
        'hy_w2': nrm(16, (DEPTH, HY_FFN, HY_FFN), HY_FFN ** -0.5),
        'hy_b2': nrm(17, (DEPTH, HY_FFN), 0.1),
        'hy_w3': nrm(18, (DEPTH, HY_FFN, HY_FFN), HY_FFN ** -0.5),
        'hy_b3': nrm(19, (DEPTH, HY_FFN), 0.1),
        'hy_w4': nrm(20, (DEPTH, HY_FFN, 2 * HY_W), HY_FFN ** -0.5),
        'hy_freq': gain(21, (DEPTH, HY_FFN)),
        'hy_bias': nrm(22, (DEPTH, HY_W), 0.5),
        'hg_lb_logits': nrm(23, (2, DEPTH, HG_W), 0.1),
        'rg_conv_w': nrm(24, (DEPTH, RG_CONV, RG_W), RG_CONV ** -0.5),
        'rg_conv_b': nrm(25, (DEPTH, RG_W), 0.02),
        'rg_wa': nrm(26, (DEPTH, 2, RG_HEADS, RG_HD, RG_HD), RG_HD ** -0.5),
        'rg_ba': nrm(27, (DEPTH, 2, RG_W), 0.1),
        'rg_wx': nrm(28, (DEPTH, 2, RG_HEADS, RG_HD, RG_HD), RG_HD ** -0.5),
        'rg_bx': nrm(29, (DEPTH, 2, RG_W), 0.1),
        'rg_lambda': rg_lambda,
        'ffn_wg': nrm(31, (DEPTH, D_MODEL, D_FF), D_MODEL ** -0.5),
        'ffn_wu': nrm(32, (DEPTH, D_MODEL, D_FF), D_MODEL ** -0.5),
        'ffn_wd': nrm(33, (DEPTH, D_FF, D_MODEL), D_FF ** -0.5),
    }


def reference(x_prompt, x_sample, c_prompt, c_sample, w_in, w_out, out_norm_w, ada_w, ada_b,
              norm1_w, norm2_w, final_norm_w, hy_conv_w, hy_conv_b, hy_w1, hy_b1, hy_w2, hy_b2,
              hy_w3, hy_b3, hy_w4, hy_freq, hy_bias, hg_lb_logits, rg_conv_w, rg_conv_b, rg_wa,
              rg_ba, rg_wx, rg_bx, rg_lambda, ffn_wg, ffn_wu, ffn_wd):
    p = dict(w_in=w_in, w_out=w_out, out_norm_w=out_norm_w, ada_w=ada_w, ada_b=ada_b,
             norm1_w=norm1_w, norm2_w=norm2_w, final_norm_w=final_norm_w,
             hy_conv_w=hy_conv_w, hy_conv_b=hy_conv_b, hy_w1=hy_w1, hy_b1=hy_b1, hy_w2=hy_w2,
             hy_b2=hy_b2, hy_w3=hy_w3, hy_b3=hy_b3, hy_w4=hy_w4, hy_freq=hy_freq, hy_bias=hy_bias,
             hg_lb_logits=hg_lb_logits, rg_conv_w=rg_conv_w, rg_conv_b=rg_conv_b, rg_wa=rg_wa,
             rg_ba=rg_ba, rg_wx=rg_wx, rg_bx=rg_bx, rg_lambda=rg_lambda,
             ffn_wg=ffn_wg, ffn_wu=ffn_wu, ffn_wd=ffn_wd)
    y_prompt = trunk(x_prompt, c_prompt, p)
    y_sample = trunk(x_sample, c_sample, p)
    return (y_prompt, y_sample)
```

```python
import functools
import math

import jax
import jax.numpy as jnp
from jax import lax
from jax.experimental import pallas as pl
from jax.experimental.pallas import tpu as pltpu

F32 = jnp.float32
BF16 = jnp.bfloat16
HIGHEST = lax.Precision.HIGHEST

D_MODEL = 1024
DEPTH = 2
HY_W = 256
HG_W = 512
RG_W = 256
HY_IN = 3 * HY_W
HG_IN = 5 * HG_W
RG_IN = 2 * RG_W
D_IN = HY_IN + HG_IN + RG_IN
HY_EMB = 33
HY_BANDS = 16
HY_FFN = 64
HY_FAST_DECAY = 0.3
HY_SLOW_DECAY = 1.5
HY_TARGET = 1e-2
HG_HEADS = 4
HG_D = 128
RG_HEADS = 4
RG_HD = 64
RG_C = 8.0
D_FF = 2816
EPS = 1e-6

LANES = 128
SUBLANES = 8
VMEM_LIMIT = 56 * 1024 * 1024

TOKEN_TILE = 512
SEQ_TILE = 512
HG_CHUNK = 64
HG_SUB = 16
DFT_N1 = 128
DFT_COLS = 8
DFT_ROWS = 512


def _params(n_axes):
    return pltpu.CompilerParams(dimension_semantics=("arbitrary",) * n_axes,
                                vmem_limit_bytes=VMEM_LIMIT)


def _dot_bf16(a, b):
    return jnp.dot(a.astype(BF16), b.astype(BF16), preferred_element_type=F32)


def _dot_f32(a, b):
    return jnp.dot(a, b, preferred_element_type=F32, precision=HIGHEST)


def _dot_nt(a, b):
    return lax.dot_general(a.astype(BF16), b.astype(BF16), (((1,), (1,)), ((), ())),
                           preferred_element_type=F32)


def _dot_tn(a, b):
    return lax.dot_general(a.astype(BF16), b.astype(BF16), (((0,), (0,)), ((), ())),
                           preferred_element_type=F32)


def _rms(x):
    return x * lax.rsqrt(jnp.mean(x * x, axis=-1, keepdims=True) + EPS)


def _sigmoid(x):
    return 1.0 / (1.0 + jnp.exp(-x))


def _row_iota(shape):
    return lax.broadcasted_iota(jnp.int32, shape, 0)


def _shift_down(x, fill_rows):
    s = len(fill_rows)
    y = pltpu.roll(x, s, 0)
    row = _row_iota(x.shape)
    for j, r in enumerate(fill_rows):
        y = jnp.where(row == j, r, y)
    return y


def _shift_up(x, next_row):
    n = x.shape[0]
    y = pltpu.roll(x, n - 1, 0)
    return jnp.where(_row_iota(x.shape) == n - 1, next_row, y)


def _ada_kernel(c_ref, w_ref, b_ref, o_ref):
    c = c_ref[...]
    s = c * _sigmoid(c)
    o_ref[0] = _dot_f32(s, w_ref[0]) + b_ref[0]


def _ada(c_all, ada_w, ada_b):
    rows = c_all.shape[0]
    n_out = ada_w.shape[-1]
    tn = n_out // 4
    return pl.pallas_call(
        _ada_kernel,
        grid=(DEPTH, n_out // tn),
        in_specs=[pl.BlockSpec((rows, D_MODEL), lambda l, j: (0, 0)),
                  pl.BlockSpec((1, D_MODEL, tn), lambda l, j: (l, 0, j)),
                  pl.BlockSpec((1, 1, tn), lambda l, j: (l, 0, j))],
        out_specs=pl.BlockSpec((1, rows, tn), lambda l, j: (l, 0, j)),
        out_shape=jax.ShapeDtypeStruct((DEPTH, rows, n_out), F32),
        compiler_params=_params(2),
        name="ada",
    )(c_all, ada_w, ada_b.reshape(DEPTH, 1, n_out))


def _mod_spec(j, tiles_per_batch):
    return pl.BlockSpec((None, None, 1, D_MODEL), lambda i: (i // tiles_per_batch, j, 0, 0))


def _inproj_kernel(x_ref, nw_ref, sh_ref, sc_ref, w_ref, ohy_ref, ohg_ref, org_ref):
    h = _rms(x_ref[...]) * nw_ref[...]
    h = (h * (1.0 + sc_ref[...]) + sh_ref[...]).astype(BF16)
    ohy_ref[...] = jnp.dot(h, w_ref[:, 0:HY_IN], preferred_element_type=F32)
    ohg_ref[...] = jnp.dot(h, w_ref[:, HY_IN:HY_IN + HG_IN], preferred_element_type=F32)
    org_ref[...] = jnp.dot(h, w_ref[:, HY_IN + HG_IN:D_IN], preferred_element_type=F32)


def _inproj(x2, mod4, norm_w, w_in_bf16, seq_len):
    t = x2.shape[0]
    tm = min(TOKEN_TILE, seq_len)
    tpb = seq_len // tm
    row = lambda i: (i, 0)
    const = lambda i: (0, 0)
    return pl.pallas_call(
        _inproj_kernel,
        grid=(t // tm,),
        in_specs=[pl.BlockSpec((tm, D_MODEL), row),
                  pl.BlockSpec((1, D_MODEL), const),
                  _mod_spec(0, tpb), _mod_spec(1, tpb),
                  pl.BlockSpec((D_MODEL, D_IN), const)],
        out_specs=[pl.BlockSpec((tm, HY_IN), row), pl.BlockSpec((tm, HG_IN), row),
                   pl.BlockSpec((tm, RG_IN), row)],
        out_shape=[jax.ShapeDtypeStruct((t, HY_IN), F32), jax.ShapeDtypeStruct((t, HG_IN), F32),
                   jax.ShapeDtypeStruct((t, RG_IN), F32)],
        compiler_params=_params(1),
        name="inproj",
    )(x2, norm_w.reshape(1, D_MODEL), mod4, mod4, w_in_bf16)


def _seq_specs(tb, width, n_tiles, reverse=False):
    r8 = tb // SUBLANES
    last8 = n_tiles * r8 - 1

    def pos(i):
        return (n_tiles - 1 - i) if reverse else i

    main = pl.BlockSpec((None, tb, width), lambda b, i: (b, pos(i), 0))
    prev = pl.BlockSpec((None, SUBLANES, width),
                        lambda b, i: (b, jnp.maximum(pos(i) * r8 - 1, 0), 0))
    nxt = pl.BlockSpec((None, SUBLANES, width),
                       lambda b, i: (b, jnp.minimum((pos(i) + 1) * r8, last8), 0))
    return main, prev, nxt


def _hy_pre_kernel(u_ref, up_ref, un_ref, cw_ref, cb_ref, v_ref, x0_ref):
    i = pl.program_id(1)
    n = pl.num_programs(1)
    x = u_ref[...]
    prev = jnp.where(i > 0, up_ref[SUBLANES - 1:SUBLANES, :], 0.0)
    nxt = jnp.where(i < n - 1, un_ref[0:1, :], 0.0)
    uc = cb_ref[...] + _shift_down(x, [prev]) * cw_ref[0:1, :]
    uc = uc + x * cw_ref[1:2, :]
    uc = uc + _shift_up(x, nxt) * cw_ref[2:3, :]
    x0_ref[...] = uc[:, 0:HY_W]
    v_ref[...] = uc[:, 2 * HY_W:3 * HY_W] * uc[:, HY_W:2 * HY_W]


def _hy_pre(u_hy, conv_w, conv_b):
    b, l, _ = u_hy.shape
    tb = min(SEQ_TILE, l)
    nt = l // tb
    main, prev, nxt = _seq_specs(tb, HY_IN, nt)
    out = pl.BlockSpec((None, tb, HY_W), lambda bb, i: (bb, i, 0))
    return pl.pallas_call(
        _hy_pre_kernel,
        grid=(b, nt),
        in_specs=[main, prev, nxt,
                  pl.BlockSpec((3, HY_IN), lambda bb, i: (0, 0)),
                  pl.BlockSpec((1, HY_IN), lambda bb, i: (0, 0))],
        out_specs=[out, out],
        out_shape=[jax.ShapeDtypeStruct((b, l, HY_W), F32)] * 2,
        compiler_params=_params(2),
        name="hy_pre",
    )(u_hy, u_hy, u_hy, conv_w, conv_b.reshape(1, HY_IN))


def _hy_filter_kernel(fb_ref, w1t_ref, w1f_ref, b1_ref, w2_ref, b2_ref, w3_ref, b3_ref, w4_ref,
                      fr_ref, dl_ref, kf_ref, kb_ref, *, seq_len, tile):
    base = pl.program_id(0) * tile
    n = (_row_iota((tile, LANES)) + base).astype(F32)
    lane = lax.broadcasted_iota(jnp.int32, (tile, LANES), 1)
    t = n / float(seq_len - 1)
    ang = fb_ref[...] * n * (2.0 * math.pi / seq_len)
    feat = jnp.where(lane < HY_BANDS, jnp.cos(ang), -jnp.sin(ang))
    fr = fr_ref[...]
    h = jnp.sin(fr * (t * w1t_ref[...] + _dot_f32(feat, w1f_ref[...]) + b1_ref[...]))
    h = jnp.sin(fr * (_dot_f32(h, w2_ref[...]) + b2_ref[...]))
    h = jnp.sin(fr * (_dot_f32(h, w3_ref[...]) + b3_ref[...]))
    k = _dot_f32(h, w4_ref[...])
    n2 = (_row_iota((tile, HY_W)) + base)
    decay = jnp.exp(-(n2.astype(F32) / float(seq_len - 1)) * dl_ref[...])
    kf_ref[...] = k[:, 0:HY_W] * decay
    kb_ref[...] = jnp.where(n2 == 0, 0.0, k[:, HY_W:2 * HY_W] * decay)


def _pad2(a, rows, cols):
    return jnp.pad(a, ((0, rows - a.shape[0]), (0, cols - a.shape[1])))


def _hy_filter(seq_len, w1, b1, w2, b2, w3, b3, w4, freq):
    tile = min(SEQ_TILE, seq_len)
    fb = jnp.linspace(1e-4, HY_BANDS - 1, HY_BANDS, dtype=F32)
    fb = _pad2(jnp.concatenate([fb, fb])[None, :], 1, LANES)
    deltas = jnp.abs(jnp.linspace(math.log(HY_TARGET) / HY_SLOW_DECAY,
                                  math.log(HY_TARGET) / HY_FAST_DECAY, HY_W, dtype=F32))[None, :]
    vec = lambda a: _pad2(a[None, :], 1, LANES)
    args = (fb, vec(w1[0]), _pad2(w1[1:], LANES, LANES), vec(b1), _pad2(w2, LANES, LANES), vec(b2),
            _pad2(w3, LANES, LANES), vec(b3), _pad2(w4, LANES, 2 * HY_W), vec(freq), deltas)
    const = lambda i: (0, 0)
    out = pl.BlockSpec((tile, HY_W), lambda i: (i, 0))
    return pl.pallas_call(
        functools.partial(_hy_filter_kernel, seq_len=seq_len, tile=tile),
        grid=(seq_len // tile,),
        in_specs=[pl.BlockSpec(a.shape, const) for a in args],
        out_specs=[out, out],
        out_shape=[jax.ShapeDtypeStruct((seq_len, HY_W), F32)] * 2,
        compiler_params=_params(1),
        name="hy_filter",
    )(*args)


def _dft_tables(seq_len):
    n = 2 * seq_len
    n1 = DFT_N1
    n2 = n // n1
    t2 = jnp.arange(n2, dtype=jnp.int32)[:, None, None]
    k1 = jnp.arange(n1, dtype=jnp.int32)[None, :, None]
    t1 = jnp.arange(n1 // 2, dtype=jnp.int32)[None, None, :]
    th = ((k1 * (n2 * t1 + t2)) % n).astype(F32) * (2.0 * math.pi / n)
    fwd = jnp.concatenate([jnp.cos(th), -jnp.sin(th)], axis=1)
    inv = jnp.swapaxes(fwd, 1, 2) * (1.0 / n)
    fwd = jnp.pad(fwd, ((0, 0), (0, 0), (0, n1 // 2)))
    a = jnp.arange(n2, dtype=jnp.int32)
    ph = ((a[:, None] * a[None, :]) % n2).astype(F32) * (2.0 * math.pi / n2)
    c, s = jnp.cos(ph), jnp.sin(ph)
    g_fwd = jnp.concatenate([jnp.concatenate([c, s], 1), jnp.concatenate([-s, c], 1)], 0)
    g_inv = jnp.concatenate([jnp.concatenate([c, -s], 1), jnp.concatenate([s, c], 1)], 0)
    return fwd, inv, g_fwd, g_inv


def _dft_a_kernel(x_ref, m_ref, y_ref, *, cols):
    half = x_ref.shape[0]
    zeros = jnp.zeros((half, HY_W), F32)
    for g in range(cols):
        sl = slice(g * HY_W, (g + 1) * HY_W)
        xz = jnp.concatenate([x_ref[:, sl], zeros], axis=0)
        y_ref[:, sl] = _dot_f32(m_ref[g], xz)


def _dft_a(x, table):
    nb, half, width = x.shape
    n2 = width // HY_W
    cols = min(DFT_COLS, n2)
    return pl.pallas_call(
        functools.partial(_dft_a_kernel, cols=cols),
        grid=(n2 // cols, nb),
        in_specs=[pl.BlockSpec((None, half, cols * HY_W), lambda j, b: (b, 0, j)),
                  pl.BlockSpec((cols, 4 * half, 2 * half), lambda j, b: (j, 0, 0))],
        out_specs=pl.BlockSpec((None, 4 * half, cols * HY_W), lambda j, b: (b, 0, j)),
        out_shape=jax.ShapeDtypeStruct((nb, 4 * half, width), F32),
        compiler_params=_params(2),
        name="hy_dft_a",
    )(x, table)


def _dft_b_filter_kernel(y_ref, g_ref, k_ref, *, n2, groups):
    g = g_ref[...]
    for j in range(groups):
        rows = slice(j * n2, (j + 1) * n2)
        xf = _dot_f32(g, jnp.concatenate([y_ref[0, 0, rows, :], y_ref[0, 1, rows, :]], axis=0))
        xb = _dot_f32(g, jnp.concatenate([y_ref[1, 0, rows, :], y_ref[1, 1, rows, :]], axis=0))
        k_ref[0, rows, :] = xf[0:n2] + xb[0:n2]
        k_ref[1, rows, :] = xf[n2:2 * n2] - xb[n2:2 * n2]


def _dft_b_filter(y, g_fwd):
    n2 = g_fwd.shape[0] // 2
    rows = y.shape[2]
    groups = max(1, DFT_ROWS // n2)
    blk = groups * n2
    return pl.pallas_call(
        functools.partial(_dft_b_filter_kernel, n2=n2, groups=groups),
        grid=(rows // blk,),
        in_specs=[pl.BlockSpec((2, 2, blk, HY_W), lambda i: (0, 0, i, 0)),
                  pl.BlockSpec(g_fwd.shape, lambda i: (0, 0))],
        out_specs=pl.BlockSpec((2, blk, HY_W), lambda i: (0, i, 0)),
        out_shape=jax.ShapeDtypeStruct((2, rows, HY_W), F32),
        compiler_params=_params(1),
        name="hy_dft_b_filter",
    )(y, g_fwd)


def _dft_b_kernel(y_ref, k_ref, gf_ref, gi_ref, u_ref, *, n2, groups):
    gf = gf_ref[...]
    gi = gi_ref[...]
    for j in range(groups):
        rows = slice(j * n2, (j + 1) * n2)
        x = _dot_f32(gf, jnp.concatenate([y_ref[0, rows, :], y_ref[1, rows, :]], axis=0))
        xr, xi = x[0:n2], x[n2:2 * n2]
        kr, ki = k_ref[0, rows, :], k_ref[1, rows, :]
        z = jnp.concatenate([xr * kr - xi * ki, xr * ki + xi * kr], axis=0)
        u = _dot_f32(gi, z)
        u_ref[0, rows, :] = u[0:n2]
        u_ref[1, rows, :] = u[n2:2 * n2]


def _dft_b(y, kspec, g_fwd, g_inv):
    nb, _, rows, _ = y.shape
    n2 = g_fwd.shape[0] // 2
    groups = max(1, DFT_ROWS // n2)
    blk = groups * n2
    return pl.pallas_call(
        functools.partial(_dft_b_kernel, n2=n2, groups=groups),
        grid=(rows // blk, nb),
        in_specs=[pl.BlockSpec((None, 2, blk, HY_W), lambda i, b: (b, 0, i, 0)),
                  pl.BlockSpec((2, blk, HY_W), lambda i, b: (0, i, 0)),
                  pl.BlockSpec(g_fwd.shape, lambda i, b: (0, 0)),
                  pl.BlockSpec(g_inv.shape, lambda i, b: (0, 0))],
        out_specs=pl.BlockSpec((None, 2, blk, HY_W), lambda i, b: (b, 0, i, 0)),
        out_shape=jax.ShapeDtypeStruct(y.shape, F32),
        compiler_params=_params(2),
        name="hy_dft_b",
    )(y, kspec, g_fwd, g_inv)


def _dft_a_inv_kernel(u_ref, m_ref, v_ref, x0_ref, bias_ref, gain_ref, o_ref, *, cols):
    bias = bias_ref[...]
    gain = gain_ref[...]
    for g in range(cols):
        sl = slice(g * HY_W, (g + 1) * HY_W)
        y = _dot_f32(m_ref[g], u_ref[:, sl])
        y = (y + v_ref[:, sl] * bias) * x0_ref[:, sl]
        o_ref[:, sl] = _rms(y) * gain


def _dft_a_inv(u, table, v, x0, bias, gain):
    nb, half, width = v.shape
    n2 = width // HY_W
    cols = min(DFT_COLS, n2)
    tile = pl.BlockSpec((None, half, cols * HY_W), lambda j, b: (b, 0, j))
    vec = pl.BlockSpec((1, HY_W), lambda j, b: (0, 0))
    return pl.pallas_call(
        functools.partial(_dft_a_inv_kernel, cols=cols),
        grid=(n2 // cols, nb),
        in_specs=[pl.BlockSpec((None, 4 * half, cols * HY_W), lambda j, b: (b, 0, j)),
                  pl.BlockSpec((cols, half, 4 * half), lambda j, b: (j, 0, 0)),
                  tile, tile, vec, vec],
        out_specs=tile,
        out_shape=jax.ShapeDtypeStruct(v.shape, F32),
        compiler_params=_params(2),
        name="hy_dft_a_inv",
    )(u, table, v, x0, bias, gain)


def _hyena(u_hy, p, l, gain, tables):
    b, seq_len, _ = u_hy.shape
    fwd, inv, g_fwd, g_inv = tables
    n1 = DFT_N1
    n2 = 2 * seq_len // n1
    v, x0 = _hy_pre(u_hy, p['hy_conv_w'][l], p['hy_conv_b'][l])
    kf, kb0 = _hy_filter(seq_len, p['hy_w1'][l], p['hy_b1'][l], p['hy_w2'][l], p['hy_b2'][l],
                         p['hy_w3'][l], p['hy_b3'][l], p['hy_w4'][l], p['hy_freq'][l])
    kk = jnp.stack([kf, kb0]).reshape(2, n1 // 2, n2 * HY_W)
    kspec = _dft_b_filter(_dft_a(kk, fwd).reshape(2, 2, n1 * n2, HY_W), g_fwd)
    vm = v.reshape(b, n1 // 2, n2 * HY_W)
    y = _dft_a(vm, fwd).reshape(b, 2, n1 * n2, HY_W)
    u = _dft_b(y, kspec, g_fwd, g_inv).reshape(b, 2 * n1, n2 * HY_W)
    out = _dft_a_inv(u, inv, vm, x0.reshape(b, n1 // 2, n2 * HY_W),
                     p['hy_bias'][l].reshape(1, HY_W), gain.reshape(1, HY_W))
    return out.reshape(b * seq_len, HY_W)


def _cumsum_rows(g):
    n = g.shape[0]
    row = _row_iota(g.shape)
    s = 1
    while s < n:
        g = g + jnp.where(row >= s, pltpu.roll(g, s, 0), 0.0)
        s *= 2
    return g


def _hg_chunk(q, v, z, lb, st, reverse):
    c = q.shape[0]
    e = jnp.exp(-z)
    s = 1.0 / (1.0 + e)
    g = jnp.log(lb + (1.0 - lb) * s)
    kk = (1.0 - lb) * (e * s)
    bq = _cumsum_rows(g)
    total = bq[c - 1:c, :]
    if reverse:
        bq = total - bq + g
    row = _row_iota((c, HG_D))
    blocks = []
    for i in range(c // HG_SUB):
        lo, hi = i * HG_SUB, (i + 1) * HG_SUB
        ref = bq[lo + HG_SUB // 2:lo + HG_SUB // 2 + 1, :]
        qt = q[lo:hi] * jnp.exp(bq[lo:hi] - ref)
        allowed = (row >= lo) if reverse else (row < hi)
        kt = jnp.where(allowed, kk * jnp.exp(ref - bq), 0.0)
        blocks.append(_dot_nt(qt, kt))
    a = jnp.concatenate(blocks, axis=0)
    r2 = _row_iota((c, c))
    c2 = lax.broadcasted_iota(jnp.int32, (c, c), 1)
    a = jnp.where((c2 >= r2) if reverse else (c2 <= r2), a, 0.0)
    o = _dot_nt(q * jnp.exp(bq), st) + _dot_bf16(a, v)
    st_new = st * jnp.exp(total) + _dot_tn(v, kk * jnp.exp(total - bq))
    return o, st_new


def _hg_kernel(*refs, reverse, combine, tb):
    if combine:
        q_ref, v_ref, z_ref, lb_ref, of_ref, g_ref, gain_ref, o_ref, st_ref, ob_ref = refs
    else:
        q_ref, v_ref, z_ref, lb_ref, o_ref, st_ref = refs
        ob_ref = o_ref

    @pl.when(pl.program_id(1) == 0)
    def _():
        st_ref[...] = jnp.zeros(st_ref.shape, F32)

    n_chunks = tb // HG_CHUNK

    def body(ci, carry):
        c = (n_chunks - 1 - ci) if reverse else ci
        r0 = pl.multiple_of(c * HG_CHUNK, HG_CHUNK)
        rows = pl.ds(r0, HG_CHUNK)
        for h in range(HG_HEADS):
            cols = slice(h * HG_D, (h + 1) * HG_D)
            o, st = _hg_chunk(q_ref[rows, cols], v_ref[rows, cols], z_ref[rows, cols],
                              lb_ref[:, cols], st_ref[h], reverse)
            st_ref[h] = st
            ob_ref[rows, cols] = o
        return carry

    lax.fori_loop(0, n_chunks, body, 0)

    if combine:
        gate = g_ref[...]
        gate = gate * _sigmoid(gate)
        for h in range(HG_HEADS):
            cols = slice(h * HG_D, (h + 1) * HG_D)
            o = of_ref[:, cols] + ob_ref[:, cols]
            o_ref[:, cols] = _rms(o) * gain_ref[:, cols] * gate[:, cols]


def _hg_direction(u_hg, lb, reverse, o_fwd=None, gain=None):
    b, l, _ = u_hg.shape
    tb = min(SEQ_TILE, l)
    nt = l // tb
    combine = o_fwd is not None

    def col(j):
        return pl.BlockSpec((None, tb, HG_W),
                            lambda bb, i: (bb, (nt - 1 - i) if reverse else i, j))

    vec = pl.BlockSpec((1, HG_W), lambda bb, i: (0, 0))
    in_specs = [col(0), col(1), col(3 if reverse else 2), vec]
    args = [u_hg, u_hg, u_hg, lb.reshape(1, HG_W)]
    scratch = [pltpu.VMEM((HG_HEADS, HG_D, HG_D), F32)]
    if combine:
        in_specs += [col(0), col(4), vec]
        args += [o_fwd, u_hg, gain.reshape(1, HG_W)]
        scratch += [pltpu.VMEM((tb, HG_W), F32)]
    return pl.pallas_call(
        functools.partial(_hg_kernel, reverse=reverse, combine=combine, tb=tb),
        grid=(b, nt),
        in_specs=in_specs,
        out_specs=col(0),
        out_shape=jax.ShapeDtypeStruct((b, l, HG_W), F32),
        scratch_shapes=scratch,
        compiler_params=_params(2),
        name="hg_bwd" if reverse else "hg_fwd",
    )(*args)


def _hgrn2(u_hg, lb_f, lb_b, gain):
    b, l, _ = u_hg.shape
    o_f = _hg_direction(u_hg, lb_f, reverse=False)
    y = _hg_direction(u_hg, lb_b, reverse=True, o_fwd=o_f, gain=gain)
    return y.reshape(b * l, HG_W)


def _scan8(a, b, reverse):
    row = _row_iota(a.shape)
    for s in (1, 2, 4):
        if reverse:
            keep = row < SUBLANES - s
            a_s, b_s = pltpu.roll(a, SUBLANES - s, 0), pltpu.roll(b, SUBLANES - s, 0)
        else:
            keep = row >= s
            a_s, b_s = pltpu.roll(a, s, 0), pltpu.roll(b, s, 0)
        b = jnp.where(keep, a * b_s, 0.0) + b
        a = jnp.where(keep, a * a_s, a)
    return a, b


def _rg_kernel(*refs, reverse, combine, tb):
    if combine:
        (u_ref, up_ref, un_ref, cw_ref, cb_ref, wh_ref, wl_ref, gb_ref, lam_ref,
         hf_ref, gain_ref, o_ref, a_ref, b_ref, carry_ref) = refs
        h_ref = b_ref
    else:
        (u_ref, up_ref, un_ref, cw_ref, cb_ref, wh_ref, wl_ref, gb_ref, lam_ref,
         o_ref, a_ref, b_ref, carry_ref) = refs
        h_ref = o_ref
    step = pl.program_id(1)
    n = pl.num_programs(1)
    pos = (n - 1 - step) if reverse else step

    @pl.when(step == 0)
    def _():
        carry_ref[...] = jnp.zeros(carry_ref.shape, F32)

    x = u_ref[:, 0:RG_W]
    has_prev = pos > 0
    has_next = pos < n - 1
    p6 = jnp.where(has_prev, up_ref[SUBLANES - 2:SUBLANES - 1, 0:RG_W], 0.0)
    p7 = jnp.where(has_prev, up_ref[SUBLANES - 1:SUBLANES, 0:RG_W], 0.0)
    nx = jnp.where(has_next, un_ref[0:1, 0:RG_W], 0.0)
    xr = cb_ref[...] + _shift_down(x, [p6, p7]) * cw_ref[0:1, :]
    xr = xr + _shift_down(x, [p7]) * cw_ref[1:2, :]
    xr = xr + x * cw_ref[2:3, :]
    xr = xr + _shift_up(x, nx) * cw_ref[3:4, :]

    x_hi = xr.astype(BF16)
    x_lo = (xr - x_hi.astype(F32)).astype(BF16)
    wh = wh_ref[...]
    pre = (jnp.dot(x_hi, wh, preferred_element_type=F32)
           + jnp.dot(x_lo, wh, preferred_element_type=F32)
           + jnp.dot(x_hi, wl_ref[...], preferred_element_type=F32)) + gb_ref[...]
    r = _sigmoid(pre[:, 0:RG_W])
    ig = _sigmoid(pre[:, RG_W:2 * RG_W])
    nl = -lam_ref[...]
    softplus = jnp.maximum(nl, 0.0) + jnp.log1p(jnp.exp(-jnp.abs(nl)))
    log_a = -RG_C * r * softplus
    a_ref[...] = jnp.exp(log_a)
    b_ref[...] = jnp.sqrt(1.0 - jnp.exp(2.0 * log_a)) * (ig * xr)

    n_groups = tb // SUBLANES

    def body(gi, carry):
        g = (n_groups - 1 - gi) if reverse else gi
        rows = pl.ds(pl.multiple_of(g * SUBLANES, SUBLANES), SUBLANES)
        a, b = _scan8(a_ref[rows, :], b_ref[rows, :], reverse)
        h = a * carry + b
        h_ref[rows, :] = h
        return h[0:1, :] if reverse else h[SUBLANES - 1:SUBLANES, :]

    carry_ref[...] = lax.fori_loop(0, n_groups, body, carry_ref[...])

    if combine:
        gate = u_ref[:, RG_W:2 * RG_W]
        gelu = 0.5 * gate * (1.0 + jnp.tanh(math.sqrt(2.0 / math.pi)
                                             * (gate + 0.044715 * (gate * gate * gate))))
        y = (hf_ref[...] + h_ref[...]) * gelu
        o_ref[...] = _rms(y) * gain_ref[...]


def _blockdiag(w):
    h, d, _ = w.shape
    eye = jnp.eye(h, dtype=w.dtype)
    return (eye[:, None, :, None] * w[:, :, None, :]).reshape(h * d, h * d)


def _rg_direction(u_rg, p, l, d, h_fwd=None, gain=None):
    b, seq_len, _ = u_rg.shape
    tb = min(SEQ_TILE, seq_len)
    nt = seq_len // tb
    reverse = d == 1
    combine = h_fwd is not None
    main, prev, nxt = _seq_specs(tb, RG_IN, nt, reverse)
    w = jnp.concatenate([_blockdiag(p['rg_wa'][l, d]), _blockdiag(p['rg_wx'][l, d])], axis=1)
    w_hi = w.astype(BF16)
    w_lo = (w - w_hi.astype(F32)).astype(BF16)
    gb = jnp.concatenate([p['rg_ba'][l, d], p['rg_bx'][l, d]]).reshape(1, 2 * RG_W)
    const = lambda bb, i: (0, 0)
    vec = pl.BlockSpec((1, RG_W), const)
    out = pl.BlockSpec((None, tb, RG_W), lambda bb, i: (bb, (nt - 1 - i) if reverse else i, 0))
    in_specs = [main, prev, nxt, pl.BlockSpec((4, RG_W), const), vec,
                pl.BlockSpec((RG_W, 2 * RG_W), const), pl.BlockSpec((RG_W, 2 * RG_W), const),
                pl.BlockSpec((1, 2 * RG_W), const), vec]
    args = [u_rg, u_rg, u_rg, p['rg_conv_w'][l], p['rg_conv_b'][l].reshape(1, RG_W), w_hi, w_lo, gb,
            p['rg_lambda'][l, d].reshape(1, RG_W)]
    if combine:
        in_specs += [out, vec]
        args += [h_fwd, gain.reshape(1, RG_W)]
    return pl.pallas_call(
        functools.partial(_rg_kernel, reverse=reverse, combine=combine, tb=tb),
        grid=(b, nt),
        in_specs=in_specs,
        out_specs=out,
        out_shape=jax.ShapeDtypeStruct((b, seq_len, RG_W), F32),
        scratch_shapes=[pltpu.VMEM((tb, RG_W), F32), pltpu.VMEM((tb, RG_W), F32),
                        pltpu.VMEM((1, RG_W), F32)],
        compiler_params=_params(2),
        name="rg_bwd" if reverse else "rg_fwd",
    )(*args)


def _rglru(u_rg, p, l, gain):
    b, seq_len, _ = u_rg.shape
    h_f = _rg_direction(u_rg, p, l, 0)
    y = _rg_direction(u_rg, p, l, 1, h_fwd=h_f, gain=gain)
    return y.reshape(b * seq_len, RG_W)


def _outffn_kernel(*refs, final, ff_chunks):
    if final:
        (x_ref, yhy_ref, yhg_ref, yrg_ref, g1_ref, sh2_ref, sc2_ref, g2_ref, n2_ref,
         wo_ref, wg_ref, wu_ref, wd_ref, fn_ref, o_ref) = refs
    else:
        (x_ref, yhy_ref, yhg_ref, yrg_ref, g1_ref, sh2_ref, sc2_ref, g2_ref, n2_ref,
         wo_ref, wg_ref, wu_ref, wd_ref, o_ref) = refs
    mix = jnp.dot(yhy_ref[...].astype(BF16), wo_ref[0:HY_W, :], preferred_element_type=F32)
    mix = mix + jnp.dot(yhg_ref[...].astype(BF16), wo_ref[HY_W:HY_W + HG_W, :],
                        preferred_element_type=F32)
    mix = mix + jnp.dot(yrg_ref[...].astype(BF16), wo_ref[HY_W + HG_W:D_MODEL, :],
                        preferred_element_type=F32)
    x1 = x_ref[...] + g1_ref[...] * mix
    h = _rms(x1) * n2_ref[...]
    h = (h * (1.0 + sc2_ref[...]) + sh2_ref[...]).astype(BF16)
    step = D_FF // ff_chunks
    ff = None
    for c in range(ff_chunks):
        sl = slice(c * step, (c + 1) * step)
        gate = jnp.dot(h, wg_ref[:, sl], preferred_element_type=F32)
        up = jnp.dot(h, wu_ref[:, sl], preferred_element_type=F32)
        act = ((gate * _sigmoid(gate)) * up).astype(BF16)
        part = jnp.dot(act, wd_ref[sl, :], preferred_element_type=F32)
        ff = part if ff is None else ff + part
    x2 = x1 + g2_ref[...] * ff
    if final:
        x2 = _rms(x2) * fn_ref[...]
    o_ref[...] = x2


def _outffn(x2, y_hy, y_hg, y_rg, mod4, norm_w, w_out, wg, wu, wd, seq_len, final_w=None):
    t = x2.shape[0]
    tm = min(TOKEN_TILE, seq_len)
    tpb = seq_len // tm
    final = final_w is not None
    row = lambda i: (i, 0)
    const = lambda i: (0, 0)

    def resident(shape):
        return pl.BlockSpec(shape, const, pipeline_mode=pl.Buffered(1))

    in_specs = [pl.BlockSpec((tm, D_MODEL), row), pl.BlockSpec((tm, HY_W), row),
                pl.BlockSpec((tm, HG_W), row), pl.BlockSpec((tm, RG_W), row),
                _mod_spec(2, tpb), _mod_spec(3, tpb), _mod_spec(4, tpb), _mod_spec(5, tpb),
                pl.BlockSpec((1, D_MODEL), const),
                resident((D_MODEL, D_MODEL)), resident((D_MODEL, D_FF)), resident((D_MODEL, D_FF)),
                resident((D_FF, D_MODEL))]
    args = [x2, y_hy, y_hg, y_rg, mod4, mod4, mod4, mod4, norm_w.reshape(1, D_MODEL),
            w_out, wg, wu, wd]
    if final:
        in_specs.append(pl.BlockSpec((1, D_MODEL), const))
        args.append(final_w.reshape(1, D_MODEL))
    return pl.pallas_call(
        functools.partial(_outffn_kernel, final=final, ff_chunks=2),
        grid=(t // tm,),
        in_specs=in_specs,
        out_specs=pl.BlockSpec((tm, D_MODEL), row),
        out_shape=jax.ShapeDtypeStruct((t, D_MODEL), F32),
        compiler_params=_params(1),
        name="outffn",
    )(*args)


def _trunk(x, mods, lb_all, p, wb):
    b, seq_len, _ = x.shape
    x2 = x.reshape(b * seq_len, D_MODEL)
    tables = _dft_tables(seq_len)
    for l in range(DEPTH):
        mod4 = mods[l].reshape(b, 6, 1, D_MODEL)
        gain = p['out_norm_w'][l]
        u_hy, u_hg, u_rg = _inproj(x2, mod4, p['norm1_w'][l], wb['w_in'][l], seq_len)
        y_hy = _hyena(u_hy.reshape(b, seq_len, HY_IN), p, l, gain[0:HY_W], tables)
        y_hg = _hgrn2(u_hg.reshape(b, seq_len, HG_IN), lb_all[0, l], lb_all[1, l],
                      gain[HY_W:HY_W + HG_W])
        y_rg = _rglru(u_rg.reshape(b, seq_len, RG_IN), p, l, gain[HY_W + HG_W:])
        x2 = _outffn(x2, y_hy, y_hg, y_rg, mod4, p['norm2_w'][l], wb['w_out'][l], wb['ffn_wg'][l],
                     wb['ffn_wu'][l], wb['ffn_wd'][l], seq_len,
                     final_w=p['final_norm_w'] if l == DEPTH - 1 else None)
    return x2.reshape(b, seq_len, D_MODEL)


def kernel(x_prompt, x_sample, c_prompt, c_sample, w_in, w_out, out_norm_w, ada_w, ada_b, norm1_w, norm2_w, final_norm_w, hy_conv_w, hy_conv_b, hy_w1, hy_b1, hy_w2, hy_b2, hy_w3, hy_b3, hy_w4, hy_freq, hy_bias, hg_lb_logits, rg_conv_w, rg_conv_b, rg_wa, rg_ba, rg_wx, rg_bx, rg_lambda, ffn_wg, ffn_wu, ffn_wd):
    p = dict(out_norm_w=out_norm_w, norm1_w=norm1_w, norm2_w=norm2_w, final_norm_w=final_norm_w,
             hy_conv_w=hy_conv_w, hy_conv_b=hy_conv_b, hy_w1=hy_w1, hy_b1=hy_b1, hy_w2=hy_w2,
             hy_b2=hy_b2, hy_w3=hy_w3, hy_b3=hy_b3, hy_w4=hy_w4, hy_freq=hy_freq, hy_bias=hy_bias,
             rg_conv_w=rg_conv_w, rg_conv_b=rg_conv_b, rg_wa=rg_wa, rg_ba=rg_ba, rg_wx=rg_wx,
             rg_bx=rg_bx, rg_lambda=rg_lambda)
    wb = dict(w_in=w_in.astype(BF16), w_out=w_out.astype(BF16), ffn_wg=ffn_wg.astype(BF16),
              ffn_wu=ffn_wu.astype(BF16), ffn_wd=ffn_wd.astype(BF16))
    lb_all = jax.nn.softmax(hg_lb_logits.astype(F32), axis=1)
    lb_all = jnp.cumsum(lb_all, axis=1) - lb_all[:, :1]
    nb_p, nb_s = c_prompt.shape[0], c_sample.shape[0]
    rows = -(-(nb_p + nb_s) // SUBLANES) * SUBLANES
    c_all = jnp.pad(jnp.concatenate([c_prompt, c_sample]), ((0, rows - nb_p - nb_s), (0, 0)))
    mods = _ada(c_all, ada_w, ada_b)
    y_prompt = _trunk(x_prompt, mods[:, 0:nb_p], lb_all, p, wb)
    y_sample = _trunk(x_sample, mods[:, nb_p:nb_p + nb_s], lb_all, p, wb)
    return (y_prompt, y_sample)
```

```python
import functools
import math

import jax
import jax.numpy as jnp
from jax import lax
from jax.experimental import pallas as pl
from jax.experimental.pallas import tpu as pltpu

F32 = jnp.float32
BF16 = jnp.bfloat16
HIGHEST = lax.Precision.HIGHEST

D_MODEL = 1024
DEPTH = 2
HY_W = 256
HG_W = 512
RG_W = 256
HY_IN = 3 * HY_W
HG_IN = 5 * HG_W
RG_IN = 2 * RG_W
D_IN = HY_IN + HG_IN + RG_IN
HY_EMB = 33
HY_BANDS = 16
HY_FFN = 64
HY_FAST_DECAY = 0.3
HY_SLOW_DECAY = 1.5
HY_TARGET = 1e-2
HG_HEADS = 4
HG_D = 128
RG_HEADS = 4
RG_HD = 64
RG_C = 8.0
D_FF = 2816
EPS = 1e-6
LOG2E = 1.4426950408889634

LANES = 128
SUBLANES = 8
VMEM_LIMIT = 56 * 1024 * 1024

TOKEN_TILE = 512
SEQ_TILE = 512
HG_CHUNK = 64
HG_SUB = 32
HG_UNROLL = 8
RG_UNROLL = 8
DFT_N1 = 128
DFT_COLS = 8
DFT_ROWS = 512


def _params(n_axes):
    return pltpu.CompilerParams(dimension_semantics=("arbitrary",) * n_axes,
                                vmem_limit_bytes=VMEM_LIMIT)


def _dot_bf16(a, b):
    return jnp.dot(a.astype(BF16), b.astype(BF16), preferred_element_type=F32)


def _dot_f32(a, b):
    return jnp.dot(a, b, preferred_element_type=F32, precision=HIGHEST)


def _dot_nt(a, b):
    return lax.dot_general(a.astype(BF16), b.astype(BF16), (((1,), (1,)), ((), ())),
                           preferred_element_type=F32)


def _dot_tn(a, b):
    return lax.dot_general(a.astype(BF16), b.astype(BF16), (((0,), (0,)), ((), ())),
                           preferred_element_type=F32)


def _rms(x):
    return x * lax.rsqrt(jnp.mean(x * x, axis=-1, keepdims=True) + EPS)


def _sigmoid(x):
    return 1.0 / (1.0 + jnp.exp(-x))


def _row_iota(shape):
    return lax.broadcasted_iota(jnp.int32, shape, 0)


def _shift_down(x, fill_rows):
    s = len(fill_rows)
    y = pltpu.roll(x, s, 0)
    head = y[0:SUBLANES]
    row = _row_iota(head.shape)
    for j, r in enumerate(fill_rows):
        head = jnp.where(row == j, r, head)
    return jnp.concatenate([head, y[SUBLANES:]], axis=0)


def _shift_up(x, next_row):
    n = x.shape[0]
    y = pltpu.roll(x, n - 1, 0)
    tail = y[n - SUBLANES:n]
    tail = jnp.where(_row_iota(tail.shape) == SUBLANES - 1, next_row, tail)
    return jnp.concatenate([y[0:n - SUBLANES], tail], axis=0)


def _ada_kernel(c_ref, w_ref, b_ref, o_ref):
    c = c_ref[...]
    s = c * _sigmoid(c)
    o_ref[0] = _dot_f32(s, w_ref[0]) + b_ref[0]


def _ada(c_all, ada_w, ada_b):
    rows = c_all.shape[0]
    n_out = ada_w.shape[-1]
    tn = n_out // 4
    return pl.pallas_call(
        _ada_kernel,
        grid=(DEPTH, n_out // tn),
        in_specs=[pl.BlockSpec((rows, D_MODEL), lambda l, j: (0, 0)),
                  pl.BlockSpec((1, D_MODEL, tn), lambda l, j: (l, 0, j)),
                  pl.BlockSpec((1, 1, tn), lambda l, j: (l, 0, j))],
        out_specs=pl.BlockSpec((1, rows, tn), lambda l, j: (l, 0, j)),
        out_shape=jax.ShapeDtypeStruct((DEPTH, rows, n_out), F32),
        compiler_params=_params(2),
        name="ada",
    )(c_all, ada_w, ada_b.reshape(DEPTH, 1, n_out))


def _mod_spec(j, tiles_per_batch):
    return pl.BlockSpec((None, None, 1, D_MODEL), lambda i: (i // tiles_per_batch, j, 0, 0))


def _inproj_kernel(x_ref, nw_ref, sh_ref, sc_ref, w_ref, ohy_ref, ohg_ref, org_ref):
    h = _rms(x_ref[...]) * nw_ref[...]
    h = (h * (1.0 + sc_ref[...]) + sh_ref[...]).astype(BF16)
    ohy_ref[...] = jnp.dot(h, w_ref[:, 0:HY_IN], preferred_element_type=F32)
    ohg_ref[...] = jnp.dot(h, w_ref[:, HY_IN:HY_IN + HG_IN], preferred_element_type=F32)
    org_ref[...] = jnp.dot(h, w_ref[:, HY_IN + HG_IN:D_IN], preferred_element_type=F32)


def _inproj(x2, mod4, norm_w, w_in_bf16, seq_len):
    t = x2.shape[0]
    tm = min(TOKEN_TILE, seq_len)
    tpb = seq_len // tm
    row = lambda i: (i, 0)
    const = lambda i: (0, 0)
    return pl.pallas_call(
        _inproj_kernel,
        grid=(t // tm,),
        in_specs=[pl.BlockSpec((tm, D_MODEL), row),
                  pl.BlockSpec((1, D_MODEL), const),
                  _mod_spec(0, tpb), _mod_spec(1, tpb),
                  pl.BlockSpec((D_MODEL, D_IN), const)],
        out_specs=[pl.BlockSpec((tm, HY_IN), row), pl.BlockSpec((tm, HG_IN), row),
                   pl.BlockSpec((tm, RG_IN), row)],
        out_shape=[jax.ShapeDtypeStruct((t, HY_IN), F32), jax.ShapeDtypeStruct((t, HG_IN), F32),
                   jax.ShapeDtypeStruct((t, RG_IN), F32)],
        compiler_params=_params(1),
        name="inproj",
    )(x2, norm_w.reshape(1, D_MODEL), mod4, mod4, w_in_bf16)


def _seq_specs(tb, width, n_tiles, reverse=False):
    r8 = tb // SUBLANES
    last8 = n_tiles * r8 - 1

    def pos(i):
        return (n_tiles - 1 - i) if reverse else i

    main = pl.BlockSpec((None, tb, width), lambda b, i: (b, pos(i), 0))
    prev = pl.BlockSpec((None, SUBLANES, width),
                        lambda b, i: (b, jnp.maximum(pos(i) * r8 - 1, 0), 0))
    nxt = pl.BlockSpec((None, SUBLANES, width),
                       lambda b, i: (b, jnp.minimum((pos(i) + 1) * r8, last8), 0))
    return main, prev, nxt


def _hy_pre_kernel(u_ref, up_ref, un_ref, cw_ref, cb_ref, v_ref, x0_ref):
    i = pl.program_id(1)
    n = pl.num_programs(1)
    x = u_ref[...]
    prev = jnp.where(i > 0, up_ref[SUBLANES - 1:SUBLANES, :], 0.0)
    nxt = jnp.where(i < n - 1, un_ref[0:1, :], 0.0)
    uc = cb_ref[...] + _shift_down(x, [prev]) * cw_ref[0:1, :]
    uc = uc + x * cw_ref[1:2, :]
    uc = uc + _shift_up(x, nxt) * cw_ref[2:3, :]
    x0_ref[...] = uc[:, 0:HY_W]
    v_ref[...] = uc[:, 2 * HY_W:3 * HY_W] * uc[:, HY_W:2 * HY_W]


def _hy_pre(u_hy, conv_w, conv_b):
    b, l, _ = u_hy.shape
    tb = min(SEQ_TILE, l)
    nt = l // tb
    main, prev, nxt = _seq_specs(tb, HY_IN, nt)
    out = pl.BlockSpec((None, tb, HY_W), lambda bb, i: (bb, i, 0))
    return pl.pallas_call(
        _hy_pre_kernel,
        grid=(b, nt),
        in_specs=[main, prev, nxt,
                  pl.BlockSpec((3, HY_IN), lambda bb, i: (0, 0)),
                  pl.BlockSpec((1, HY_IN), lambda bb, i: (0, 0))],
        out_specs=[out, out],
        out_shape=[jax.ShapeDtypeStruct((b, l, HY_W), F32)] * 2,
        compiler_params=_params(2),
        name="hy_pre",
    )(u_hy, u_hy, u_hy, conv_w, conv_b.reshape(1, HY_IN))


def _hy_filter_kernel(fb_ref, w1t_ref, w1f_ref, b1_ref, w2_ref, b2_ref, w3_ref, b3_ref, w4_ref,
                      fr_ref, dl_ref, k_ref, *, seq_len, tile):
    base = pl.program_id(0) * tile
    n = (_row_iota((tile, LANES)) + base).astype(F32)
    lane = lax.broadcasted_iota(jnp.int32, (tile, LANES), 1)
    t = n / float(seq_len - 1)
    ang = fb_ref[...] * n * (2.0 * math.pi / seq_len)
    feat = jnp.where(lane < HY_BANDS, jnp.cos(ang), -jnp.sin(ang))
    fr = fr_ref[...]
    h = jnp.sin(fr * (t * w1t_ref[...] + _dot_f32(feat, w1f_ref[...]) + b1_ref[...]))
    h = jnp.sin(fr * (_dot_f32(h, w2_ref[...]) + b2_ref[...]))
    h = jnp.sin(fr * (_dot_f32(h, w3_ref[...]) + b3_ref[...]))
    k = _dot_f32(h, w4_ref[...])
    n2 = (_row_iota((tile, HY_W)) + base)
    decay = jnp.exp(-(n2.astype(F32) / float(seq_len - 1)) * dl_ref[...])
    k_ref[0] = k[:, 0:HY_W] * decay
    k_ref[1] = jnp.where(n2 == 0, 0.0, k[:, HY_W:2 * HY_W] * decay)


def _pad2(a, rows, cols):
    return jnp.pad(a, ((0, rows - a.shape[0]), (0, cols - a.shape[1])))


def _hy_filter(seq_len, w1, b1, w2, b2, w3, b3, w4, freq):
    tile = min(SEQ_TILE, seq_len)
    fb = jnp.linspace(1e-4, HY_BANDS - 1, HY_BANDS, dtype=F32)
    fb = _pad2(jnp.concatenate([fb, fb])[None, :], 1, LANES)
    deltas = jnp.abs(jnp.linspace(math.log(HY_TARGET) / HY_SLOW_DECAY,
                                  math.log(HY_TARGET) / HY_FAST_DECAY, HY_W, dtype=F32))[None, :]
    vec = lambda a: _pad2(a[None, :], 1, LANES)
    args = (fb, vec(w1[0]), _pad2(w1[1:], LANES, LANES), vec(b1), _pad2(w2, LANES, LANES), vec(b2),
            _pad2(w3, LANES, LANES), vec(b3), _pad2(w4, LANES, 2 * HY_W), vec(freq), deltas)
    const = lambda i: (0, 0)
    return pl.pallas_call(
        functools.partial(_hy_filter_kernel, seq_len=seq_len, tile=tile),
        grid=(seq_len // tile,),
        in_specs=[pl.BlockSpec(a.shape, const) for a in args],
        out_specs=pl.BlockSpec((2, tile, HY_W), lambda i: (0, i, 0)),
        out_shape=jax.ShapeDtypeStruct((2, seq_len, HY_W), F32),
        compiler_params=_params(1),
        name="hy_filter",
    )(*args)


def _dft_tables(seq_len):
    n = 2 * seq_len
    n1 = DFT_N1
    n2 = n // n1
    t2 = jnp.arange(n2, dtype=jnp.int32)[:, None, None]
    k1 = jnp.arange(n1, dtype=jnp.int32)[None, :, None]
    t1 = jnp.arange(n1 // 2, dtype=jnp.int32)[None, None, :]
    th = ((k1 * (n2 * t1 + t2)) % n).astype(F32) * (2.0 * math.pi / n)
    fwd = jnp.concatenate([jnp.cos(th), -jnp.sin(th)], axis=1)
    inv = jnp.swapaxes(fwd, 1, 2) * (1.0 / n)
    a = jnp.arange(n2, dtype=jnp.int32)
    ph = ((a[:, None] * a[None, :]) % n2).astype(F32) * (2.0 * math.pi / n2)
    c, s = jnp.cos(ph), jnp.sin(ph)
    g_fwd = jnp.concatenate([jnp.concatenate([c, s], 1), jnp.concatenate([-s, c], 1)], 0)
    g_inv = jnp.concatenate([jnp.concatenate([c, -s], 1), jnp.concatenate([s, c], 1)], 0)
    return dict(fwd=_split(fwd), inv=inv.astype(BF16), g_fwd=_split(g_fwd), g_inv=g_inv.astype(BF16))


def _split(x):
    hi = x.astype(BF16)
    return hi, (x - hi.astype(F32)).astype(BF16)


def _dot_table(t_hi, t_lo, x, precise):
    if not precise:
        return jnp.dot(t_hi, x.astype(BF16), preferred_element_type=F32)
    x_hi, x_lo = _split(x)
    return (jnp.dot(t_hi, x_hi, preferred_element_type=F32)
            + jnp.dot(t_lo, x_hi, preferred_element_type=F32)
            + jnp.dot(t_hi, x_lo, preferred_element_type=F32))


def _dft_a_kernel(x_ref, mh_ref, ml_ref, y_ref, *, cols, precise):
    for g in range(cols):
        y_ref[:, g, :] = _dot_table(mh_ref[g], ml_ref[g], x_ref[:, g, :], precise)


def _dft_a(x, table, precise):
    nb, half, n2, _ = x.shape
    cols = min(DFT_COLS, n2)
    tab = pl.BlockSpec((cols, 4 * half, half), lambda j, b: (j, 0, 0))
    return pl.pallas_call(
        functools.partial(_dft_a_kernel, cols=cols, precise=precise),
        grid=(n2 // cols, nb),
        in_specs=[pl.BlockSpec((None, half, cols, HY_W), lambda j, b: (b, 0, j, 0)), tab, tab],
        out_specs=pl.BlockSpec((None, 4 * half, cols, HY_W), lambda j, b: (b, 0, j, 0)),
        out_shape=jax.ShapeDtypeStruct((nb, 4 * half, n2, HY_W), F32),
        compiler_params=_params(2),
        name="hy_dft_a",
    )(x, table[0], table[1])


def _dft_b_filter_kernel(y_ref, gh_ref, gl_ref, k_ref, *, groups):
    gh, gl = gh_ref[...], gl_ref[...]
    n2 = y_ref.shape[-2]
    for j in range(groups):
        xf = _dot_table(gh, gl, jnp.concatenate([y_ref[0, 0, j], y_ref[0, 1, j]], axis=0), True)
        xb = _dot_table(gh, gl, jnp.concatenate([y_ref[1, 0, j], y_ref[1, 1, j]], axis=0), True)
        k_ref[0, j] = xf[0:n2] + xb[0:n2]
        k_ref[1, j] = xf[n2:2 * n2] - xb[n2:2 * n2]


def _dft_b_filter(y, g_fwd):
    _, _, n1, n2, _ = y.shape
    groups = max(1, DFT_ROWS // n2)
    const = lambda i: (0, 0)
    return pl.pallas_call(
        functools.partial(_dft_b_filter_kernel, groups=groups),
        grid=(n1 // groups,),
        in_specs=[pl.BlockSpec((2, 2, groups, n2, HY_W), lambda i: (0, 0, i, 0, 0)),
                  pl.BlockSpec((2 * n2, 2 * n2), const), pl.BlockSpec((2 * n2, 2 * n2), const)],
        out_specs=pl.BlockSpec((2, groups, n2, HY_W), lambda i: (0, i, 0, 0)),
        out_shape=jax.ShapeDtypeStruct((2, n1, n2, HY_W), F32),
        compiler_params=_params(1),
        name="hy_dft_b_filter",
    )(y, g_fwd[0], g_fwd[1])


def _dft_b_kernel(y_ref, k_ref, gf_ref, gi_ref, u_ref, *, groups):
    gf = gf_ref[...]
    gi = gi_ref[...]
    n2 = y_ref.shape[-2]
    for j in range(groups):
        x = _dot_table(gf, None, jnp.concatenate([y_ref[0, j], y_ref[1, j]], axis=0), False)
        xr, xi = x[0:n2], x[n2:2 * n2]
        kr, ki = k_ref[0, j], k_ref[1, j]
        z = jnp.concatenate([xr * kr - xi * ki, xr * ki + xi * kr], axis=0)
        u = _dot_table(gi, None, z, False)
        u_ref[0, j] = u[0:n2]
        u_ref[1, j] = u[n2:2 * n2]


def _dft_b(y, kspec, g_fwd, g_inv):
    nb, _, n1, n2, _ = y.shape
    groups = max(1, DFT_ROWS // n2)
    const = lambda i, b: (0, 0)
    return pl.pallas_call(
        functools.partial(_dft_b_kernel, groups=groups),
        grid=(n1 // groups, nb),
        in_specs=[pl.BlockSpec((None, 2, groups, n2, HY_W), lambda i, b: (b, 0, i, 0, 0)),
                  pl.BlockSpec((2, groups, n2, HY_W), lambda i, b: (0, i, 0, 0)),
                  pl.BlockSpec((2 * n2, 2 * n2), const), pl.BlockSpec((2 * n2, 2 * n2), const)],
        out_specs=pl.BlockSpec((None, 2, groups, n2, HY_W), lambda i, b: (b, 0, i, 0, 0)),
        out_shape=jax.ShapeDtypeStruct(y.shape, F32),
        compiler_params=_params(2),
        name="hy_dft_b",
    )(y, kspec, g_fwd, g_inv)


def _dft_a_inv_kernel(u_ref, m_ref, v_ref, x0_ref, bias_ref, gain_ref, o_ref, *, cols):
    bias = bias_ref[...]
    gain = gain_ref[...]
    for g in range(cols):
        y = _dot_table(m_ref[g], None, u_ref[:, g, :], False)
        y = (y + v_ref[:, g, :] * bias) * x0_ref[:, g, :]
        o_ref[:, g, :] = _rms(y) * gain


def _dft_a_inv(u, table, v, x0, bias, gain):
    nb, half, n2, _ = v.shape
    cols = min(DFT_COLS, n2)
    tile = pl.BlockSpec((None, half, cols, HY_W), lambda j, b: (b, 0, j, 0))
    vec = pl.BlockSpec((1, HY_W), lambda j, b: (0, 0))
    return pl.pallas_call(
        functools.partial(_dft_a_inv_kernel, cols=cols),
        grid=(n2 // cols, nb),
        in_specs=[pl.BlockSpec((None, 4 * half, cols, HY_W), lambda j, b: (b, 0, j, 0)),
                  pl.BlockSpec((cols, half, 4 * half), lambda j, b: (j, 0, 0)),
                  tile, tile, vec, vec],
        out_specs=tile,
        out_shape=jax.ShapeDtypeStruct(v.shape, F32),
        compiler_params=_params(2),
        name="hy_dft_a_inv",
    )(u, table, v, x0, bias, gain)


def _hyena(u_hy, p, l, gain, tables):
    b, seq_len, _ = u_hy.shape
    n1 = DFT_N1
    n2 = 2 * seq_len // n1
    v, x0 = _hy_pre(u_hy, p['hy_conv_w'][l], p['hy_conv_b'][l])
    kk = _hy_filter(seq_len, p['hy_w1'][l], p['hy_b1'][l], p['hy_w2'][l], p['hy_b2'][l],
                    p['hy_w3'][l], p['hy_b3'][l], p['hy_w4'][l], p['hy_freq'][l])
    ky = _dft_a(kk.reshape(2, n1 // 2, n2, HY_W), tables['fwd'], True)
    kspec = _dft_b_filter(ky.reshape(2, 2, n1, n2, HY_W), tables['g_fwd'])
    vm = v.reshape(b, n1 // 2, n2, HY_W)
    y = _dft_a(vm, tables['fwd'], False).reshape(b, 2, n1, n2, HY_W)
    u = _dft_b(y, kspec, tables['g_fwd'][0], tables['g_inv']).reshape(b, 2 * n1, n2, HY_W)
    out = _dft_a_inv(u, tables['inv'], vm, x0.reshape(b, n1 // 2, n2, HY_W),
                     p['hy_bias'][l].reshape(1, HY_W), gain.reshape(1, HY_W))
    return out.reshape(b * seq_len, HY_W)


def _cumsum_rows(g):
    n = g.shape[0]
    row = _row_iota((SUBLANES, g.shape[1]))
    s = 1
    while s < n:
        if s < SUBLANES:
            y = pltpu.roll(g, s, 0)
            head = jnp.where(row >= s, y[0:SUBLANES], 0.0)
            g = g + jnp.concatenate([head, y[SUBLANES:]], axis=0)
        else:
            g = jnp.concatenate([g[0:s], g[s:] + g[0:n - s]], axis=0)
        s *= 2
    return g


def _hg_chunk(q, v, z, lb, st, reverse):
    c = q.shape[0]
    e = jnp.exp2(z * (-LOG2E))
    s = 1.0 / (1.0 + e)
    g = jnp.log2(lb + (1.0 - lb) * s)
    kk = (1.0 - lb) * (e * s)
    bq = _cumsum_rows(g)
    total = bq[c - 1:c, :]
    if reverse:
        bq = total - bq + g
    blocks = []
    for i in range(c // HG_SUB):
        lo, hi = i * HG_SUB, (i + 1) * HG_SUB
        ks, ke = (lo, c) if reverse else (0, hi)
        ref = bq[lo + HG_SUB // 2:lo + HG_SUB // 2 + 1, :]
        qt = q[lo:hi] * jnp.exp2(bq[lo:hi] - ref)
        kt = kk[ks:ke] * jnp.exp2(ref - bq[ks:ke])
        parts = ([jnp.zeros((ks, HG_D), F32)] if ks else []) + [kt]
        parts += [jnp.zeros((c - ke, HG_D), F32)] if ke < c else []
        blocks.append(_dot_nt(qt, jnp.concatenate(parts, axis=0)))
    a = jnp.concatenate(blocks, axis=0)
    r2 = _row_iota((c, c))
    c2 = lax.broadcasted_iota(jnp.int32, (c, c), 1)
    a = jnp.where((c2 >= r2) if reverse else (c2 <= r2), a, 0.0)
    o = _dot_nt(q * jnp.exp2(bq), st) + _dot_bf16(a, v)
    st_new = st * jnp.exp2(total) + _dot_tn(v, kk * jnp.exp2(total - bq))
    return o, st_new


def _hg_kernel(*refs, reverse, combine, tb):
    if combine:
        q_ref, v_ref, z_ref, lb_ref, of_ref, g_ref, gain_ref, o_ref, st_ref, ob_ref = refs
    else:
        q_ref, v_ref, z_ref, lb_ref, o_ref, st_ref = refs
        ob_ref = o_ref

    @pl.when(pl.program_id(1) == 0)
    def _():
        st_ref[...] = jnp.zeros(st_ref.shape, F32)

    n_chunks = tb // HG_CHUNK

    def body(ci, carry):
        c = (n_chunks - 1 - ci) if reverse else ci
        r0 = pl.multiple_of(c * HG_CHUNK, HG_CHUNK)
        rows = pl.ds(r0, HG_CHUNK)
        for h in range(HG_HEADS):
            cols = slice(h * HG_D, (h + 1) * HG_D)
            o, st = _hg_chunk(q_ref[rows, cols], v_ref[rows, cols], z_ref[rows, cols],
                              lb_ref[:, cols], st_ref[h], reverse)
            st_ref[h] = st
            ob_ref[rows, cols] = o
        return carry

    lax.fori_loop(0, n_chunks, body, 0, unroll=HG_UNROLL)

    if combine:
        gate = g_ref[...]
        gate = gate * _sigmoid(gate)
        for h in range(HG_HEADS):
            cols = slice(h * HG_D, (h + 1) * HG_D)
            o = of_ref[:, cols] + ob_ref[:, cols]
            o_ref[:, cols] = _rms(o) * gain_ref[:, cols] * gate[:, cols]


def _hg_direction(u_hg, lb, reverse, o_fwd=None, gain=None):
    b, l, _ = u_hg.shape
    tb = min(SEQ_TILE, l)
    nt = l // tb
    combine = o_fwd is not None

    def col(j):
        return pl.BlockSpec((None, tb, HG_W),
                            lambda bb, i: (bb, (nt - 1 - i) if reverse else i, j))

    vec = pl.BlockSpec((1, HG_W), lambda bb, i: (0, 0))
    in_specs = [col(0), col(1), col(3 if reverse else 2), vec]
    args = [u_hg, u_hg, u_hg, lb.reshape(1, HG_W)]
    scratch = [pltpu.VMEM((HG_HEADS, HG_D, HG_D), F32)]
    if combine:
        in_specs += [col(0), col(4), vec]
        args += [o_fwd, u_hg, gain.reshape(1, HG_W)]
        scratch += [pltpu.VMEM((tb, HG_W), F32)]
    return pl.pallas_call(
        functools.partial(_hg_kernel, reverse=reverse, combine=combine, tb=tb),
        grid=(b, nt),
        in_specs=in_specs,
        out_specs=col(0),
        out_shape=jax.ShapeDtypeStruct((b, l, HG_W), F32),
        scratch_shapes=scratch,
        compiler_params=_params(2),
        name="hg_bwd" if reverse else "hg_fwd",
    )(*args)


def _hgrn2(u_hg, lb_f, lb_b, gain):
    b, l, _ = u_hg.shape
    o_f = _hg_direction(u_hg, lb_f, reverse=False)
    y = _hg_direction(u_hg, lb_b, reverse=True, o_fwd=o_f, gain=gain)
    return y.reshape(b * l, HG_W)


def _scan8(a, b, reverse):
    row = _row_iota(a.shape)
    for s in (1, 2, 4):
        if reverse:
            keep = row < SUBLANES - s
            a_s, b_s = pltpu.roll(a, SUBLANES - s, 0), pltpu.roll(b, SUBLANES - s, 0)
        else:
            keep = row >= s
            a_s, b_s = pltpu.roll(a, s, 0), pltpu.roll(b, s, 0)
        b = jnp.where(keep, a * b_s, 0.0) + b
        a = jnp.where(keep, a * a_s, a)
    return a, b


def _rg_kernel(*refs, reverse, combine, tb):
    if combine:
        (u_ref, up_ref, un_ref, cw_ref, cb_ref, wh_ref, wl_ref, gb_ref, lam_ref,
         hf_ref, gain_ref, o_ref, a_ref, b_ref, carry_ref) = refs
        h_ref = b_ref
    else:
        (u_ref, up_ref, un_ref, cw_ref, cb_ref, wh_ref, wl_ref, gb_ref, lam_ref,
         o_ref, a_ref, b_ref, carry_ref) = refs
        h_ref = o_ref
    step = pl.program_id(1)
    n = pl.num_programs(1)
    pos = (n - 1 - step) if reverse else step

    @pl.when(step == 0)
    def _():
        carry_ref[...] = jnp.zeros(carry_ref.shape, F32)

    x = u_ref[:, 0:RG_W]
    has_prev = pos > 0
    has_next = pos < n - 1
    p6 = jnp.where(has_prev, up_ref[SUBLANES - 2:SUBLANES - 1, 0:RG_W], 0.0)
    p7 = jnp.where(has_prev, up_ref[SUBLANES - 1:SUBLANES, 0:RG_W], 0.0)
    nx = jnp.where(has_next, un_ref[0:1, 0:RG_W], 0.0)
    xr = cb_ref[...] + _shift_down(x, [p6, p7]) * cw_ref[0:1, :]
    xr = xr + _shift_down(x, [p7]) * cw_ref[1:2, :]
    xr = xr + x * cw_ref[2:3, :]
    xr = xr + _shift_up(x, nx) * cw_ref[3:4, :]

    x_hi = xr.astype(BF16)
    x_lo = (xr - x_hi.astype(F32)).astype(BF16)
    wh = wh_ref[...]
    pre = (jnp.dot(x_hi, wh, preferred_element_type=F32)
           + jnp.dot(x_lo, wh, preferred_element_type=F32)
           + jnp.dot(x_hi, wl_ref[...], preferred_element_type=F32)) + gb_ref[...]
    r = _sigmoid(pre[:, 0:RG_W])
    ig = _sigmoid(pre[:, RG_W:2 * RG_W])
    nl = -lam_ref[...]
    softplus = jnp.maximum(nl, 0.0) + jnp.log1p(jnp.exp(-jnp.abs(nl)))
    a = jnp.exp(-RG_C * r * softplus)
    a_ref[...] = a
    b_ref[...] = jnp.sqrt(1.0 - a * a) * (ig * xr)

    n_groups = tb // SUBLANES

    def body(gi, carry):
        g = (n_groups - 1 - gi) if reverse else gi
        rows = pl.ds(pl.multiple_of(g * SUBLANES, SUBLANES), SUBLANES)
        a, b = _scan8(a_ref[rows, :], b_ref[rows, :], reverse)
        h = a * carry + b
        h_ref[rows, :] = h
        return h[0:1, :] if reverse else h[SUBLANES - 1:SUBLANES, :]

    carry_ref[...] = lax.fori_loop(0, n_groups, body, carry_ref[...], unroll=RG_UNROLL)

    if combine:
        gate = u_ref[:, RG_W:2 * RG_W]
        gelu = 0.5 * gate * (1.0 + jnp.tanh(math.sqrt(2.0 / math.pi)
                                             * (gate + 0.044715 * (gate * gate * gate))))
        y = (hf_ref[...] + h_ref[...]) * gelu
        o_ref[...] = _rms(y) * gain_ref[...]


def _blockdiag(w):
    h, d, _ = w.shape
    eye = jnp.eye(h, dtype=w.dtype)
    return (eye[:, None, :, None] * w[:, :, None, :]).reshape(h * d, h * d)


def _rg_direction(u_rg, p, l, d, h_fwd=None, gain=None):
    b, seq_len, _ = u_rg.shape
    tb = min(SEQ_TILE, seq_len)
    nt = seq_len // tb
    reverse = d == 1
    combine = h_fwd is not None
    main, prev, nxt = _seq_specs(tb, RG_IN, nt, reverse)
    w = jnp.concatenate([_blockdiag(p['rg_wa'][l, d]), _blockdiag(p['rg_wx'][l, d])], axis=1)
    w_hi = w.astype(BF16)
    w_lo = (w - w_hi.astype(F32)).astype(BF16)
    gb = jnp.concatenate([p['rg_ba'][l, d], p['rg_bx'][l, d]]).reshape(1, 2 * RG_W)
    const = lambda bb, i: (0, 0)
    vec = pl.BlockSpec((1, RG_W), const)
    out = pl.BlockSpec((None, tb, RG_W), lambda bb, i: (bb, (nt - 1 - i) if reverse else i, 0))
    in_specs = [main, prev, nxt, pl.BlockSpec((4, RG_W), const), vec,
                pl.BlockSpec((RG_W, 2 * RG_W), const), pl.BlockSpec((RG_W, 2 * RG_W), const),
                pl.BlockSpec((1, 2 * RG_W), const), vec]
    args = [u_rg, u_rg, u_rg, p['rg_conv_w'][l], p['rg_conv_b'][l].reshape(1, RG_W), w_hi, w_lo, gb,
            p['rg_lambda'][l, d].reshape(1, RG_W)]
    if combine:
        in_specs += [out, vec]
        args += [h_fwd, gain.reshape(1, RG_W)]
    return pl.pallas_call(
        functools.partial(_rg_kernel, reverse=reverse, combine=combine, tb=tb),
        grid=(b, nt),
        in_specs=in_specs,
        out_specs=out,
        out_shape=jax.ShapeDtypeStruct((b, seq_len, RG_W), F32),
        scratch_shapes=[pltpu.VMEM((tb, RG_W), F32), pltpu.VMEM((tb, RG_W), F32),
                        pltpu.VMEM((1, RG_W), F32)],
        compiler_params=_params(2),
        name="rg_bwd" if reverse else "rg_fwd",
    )(*args)


def _rglru(u_rg, p, l, gain):
    b, seq_len, _ = u_rg.shape
    h_f = _rg_direction(u_rg, p, l, 0)
    y = _rg_direction(u_rg, p, l, 1, h_fwd=h_f, gain=gain)
    return y.reshape(b * seq_len, RG_W)


def _outffn_kernel(*refs, final, ff_chunks):
    if final:
        (x_ref, yhy_ref, yhg_ref, yrg_ref, g1_ref, sh2_ref, sc2_ref, g2_ref, n2_ref,
         wo_ref, wg_ref, wu_ref, wd_ref, fn_ref, o_ref) = refs
    else:
        (x_ref, yhy_ref, yhg_ref, yrg_ref, g1_ref, sh2_ref, sc2_ref, g2_ref, n2_ref,
         wo_ref, wg_ref, wu_ref, wd_ref, o_ref) = refs
    mix = jnp.dot(yhy_ref[...].astype(BF16), wo_ref[0:HY_W, :], preferred_element_type=F32)
    mix = mix + jnp.dot(yhg_ref[...].astype(BF16), wo_ref[HY_W:HY_W + HG_W, :],
                        preferred_element_type=F32)
    mix = mix + jnp.dot(yrg_ref[...].astype(BF16), wo_ref[HY_W + HG_W:D_MODEL, :],
                        preferred_element_type=F32)
    x1 = x_ref[...] + g1_ref[...] * mix
    h = _rms(x1) * n2_ref[...]
    h = (h * (1.0 + sc2_ref[...]) + sh2_ref[...]).astype(BF16)
    step = D_FF // ff_chunks
    ff = None
    for c in range(ff_chunks):
        sl = slice(c * step, (c + 1) * step)
        gate = jnp.dot(h, wg_ref[:, sl], preferred_element_type=F32)
        up = jnp.dot(h, wu_ref[:, sl], preferred_element_type=F32)
        act = ((gate * _sigmoid(gate)) * up).astype(BF16)
        part = jnp.dot(act, wd_ref[sl, :], preferred_element_type=F32)
        ff = part if ff is None else ff + part
    x2 = x1 + g2_ref[...] * ff
    if final:
        x2 = _rms(x2) * fn_ref[...]
    o_ref[...] = x2


def _outffn(x2, y_hy, y_hg, y_rg, mod4, norm_w, w_out, wg, wu, wd, seq_len, final_w=None):
    t = x2.shape[0]
    tm = min(TOKEN_TILE, seq_len)
    tpb = seq_len // tm
    final = final_w is not None
    row = lambda i: (i, 0)
    const = lambda i: (0, 0)

    def resident(shape):
        return pl.BlockSpec(shape, const, pipeline_mode=pl.Buffered(1))

    in_specs = [pl.BlockSpec((tm, D_MODEL), row), pl.BlockSpec((tm, HY_W), row),
                pl.BlockSpec((tm, HG_W), row), pl.BlockSpec((tm, RG_W), row),
                _mod_spec(2, tpb), _mod_spec(3, tpb), _mod_spec(4, tpb), _mod_spec(5, tpb),
                pl.BlockSpec((1, D_MODEL), const),
                resident((D_MODEL, D_MODEL)), resident((D_MODEL, D_FF)), resident((D_MODEL, D_FF)),
                resident((D_FF, D_MODEL))]
    args = [x2, y_hy, y_hg, y_rg, mod4, mod4, mod4, mod4, norm_w.reshape(1, D_MODEL),
            w_out, wg, wu, wd]
    if final:
        in_specs.append(pl.BlockSpec((1, D_MODEL), const))
        args.append(final_w.reshape(1, D_MODEL))
    return pl.pallas_call(
        functools.partial(_outffn_kernel, final=final, ff_chunks=2),
        grid=(t // tm,),
        in_specs=in_specs,
        out_specs=pl.BlockSpec((tm, D_MODEL), row),
        out_shape=jax.ShapeDtypeStruct((t, D_MODEL), F32),
        compiler_params=_params(1),
        name="outffn",
    )(*args)


def _trunk(x, mods, lb_all, p, wb):
    b, seq_len, _ = x.shape
    x2 = x.reshape(b * seq_len, D_MODEL)
    tables = _dft_tables(seq_len)
    for l in range(DEPTH):
        mod4 = mods[l].reshape(b, 6, 1, D_MODEL)
        gain = p['out_norm_w'][l]
        u_hy, u_hg, u_rg = _inproj(x2, mod4, p['norm1_w'][l], wb['w_in'][l], seq_len)
        y_hy = _hyena(u_hy.reshape(b, seq_len, HY_IN), p, l, gain[0:HY_W], tables)
        y_hg = _hgrn2(u_hg.reshape(b, seq_len, HG_IN), lb_all[0, l], lb_all[1, l],
                      gain[HY_W:HY_W + HG_W])
        y_rg = _rglru(u_rg.reshape(b, seq_len, RG_IN), p, l, gain[HY_W + HG_W:])
        x2 = _outffn(x2, y_hy, y_hg, y_rg, mod4, p['norm2_w'][l], wb['w_out'][l], wb['ffn_wg'][l],
                     wb['ffn_wu'][l], wb['ffn_wd'][l], seq_len,
                     final_w=p['final_norm_w'] if l == DEPTH - 1 else None)
    return x2.reshape(b, seq_len, D_MODEL)


def kernel(x_prompt, x_sample, c_prompt, c_sample, w_in, w_out, out_norm_w, ada_w, ada_b, norm1_w, norm2_w, final_norm_w, hy_conv_w, hy_conv_b, hy_w1, hy_b1, hy_w2, hy_b2, hy_w3, hy_b3, hy_w4, hy_freq, hy_bias, hg_lb_logits, rg_conv_w, rg_conv_b, rg_wa, rg_ba, rg_wx, rg_bx, rg_lambda, ffn_wg, ffn_wu, ffn_wd):
    p = dict(out_norm_w=out_norm_w, norm1_w=norm1_w, norm2_w=norm2_w, final_norm_w=final_norm_w,
             hy_conv_w=hy_conv_w, hy_conv_b=hy_conv_b, hy_w1=hy_w1, hy_b1=hy_b1, hy_w2=hy_w2,
             hy_b2=hy_b2, hy_w3=hy_w3, hy_b3=hy_b3, hy_w4=hy_w4, hy_freq=hy_freq, hy_bias=hy_bias,
             rg_conv_w=rg_conv_w, rg_conv_b=rg_conv_b, rg_wa=rg_wa, rg_ba=rg_ba, rg_wx=rg_wx,
             rg_bx=rg_bx, rg_lambda=rg_lambda)
    wb = dict(w_in=w_in.astype(BF16), w_out=w_out.astype(BF16), ffn_wg=ffn_wg.astype(BF16),
              ffn_wu=ffn_wu.astype(BF16), ffn_wd=ffn_wd.astype(BF16))
    lb_all = jax.nn.softmax(hg_lb_logits.astype(F32), axis=1)
    lb_all = jnp.cumsum(lb_all, axis=1) - lb_all[:, :1]
    nb_p, nb_s = c_prompt.shape[0], c_sample.shape[0]
    rows = -(-(nb_p + nb_s) // SUBLANES) * SUBLANES
    c_all = jnp.pad(jnp.concatenate([c_prompt, c_sample]), ((0, rows - nb_p - nb_s), (0, 0)))
    mods = _ada(c_all, ada_w, ada_b)
    y_prompt = _trunk(x_prompt, mods[:, 0:nb_p], lb_all, p, wb)
    y_sample = _trunk(x_sample, mods[:, nb_p:nb_p + nb_s], lb_all, p, wb)
    return (y_prompt, y_sample)
```

```python
import functools
import math

import jax
import jax.numpy as jnp
from jax import lax
from jax.experimental import pallas as pl
from jax.experimental.pallas import tpu as pltpu

F32 = jnp.float32
BF16 = jnp.bfloat16
HIGHEST = lax.Precision.HIGHEST

D_MODEL = 1024
DEPTH = 2
HY_W = 256
HG_W = 512
RG_W = 256
HY_IN = 3 * HY_W
HG_IN = 5 * HG_W
RG_IN = 2 * RG_W
D_IN = HY_IN + HG_IN + RG_IN
HY_EMB = 33
HY_BANDS = 16
HY_FFN = 64
HY_FAST_DECAY = 0.3
HY_SLOW_DECAY = 1.5
HY_TARGET = 1e-2
HG_HEADS = 4
HG_D = 128
RG_HEADS = 4
RG_HD = 64
RG_C = 8.0
D_FF = 2816
EPS = 1e-6
LOG2E = 1.4426950408889634

LANES = 128
SUBLANES = 8
VMEM_LIMIT = 56 * 1024 * 1024

TOKEN_TILE = 512
SEQ_TILE = 512
HG_CHUNK = 64
HG_SUB = 32
HG_UNROLL = 8
RG_UNROLL = 8
DFT_N1 = 128
DFT_ROWS = 2048


def _params(n_axes):
    return pltpu.CompilerParams(dimension_semantics=("arbitrary",) * n_axes,
                                vmem_limit_bytes=VMEM_LIMIT)


def _dot_bf16(a, b):
    return jnp.dot(a.astype(BF16), b.astype(BF16), preferred_element_type=F32)


def _dot_f32(a, b):
    return jnp.dot(a, b, preferred_element_type=F32, precision=HIGHEST)


def _dot_nt(a, b):
    return lax.dot_general(a.astype(BF16), b.astype(BF16), (((1,), (1,)), ((), ())),
                           preferred_element_type=F32)


def _dot_tn(a, b):
    return lax.dot_general(a.astype(BF16), b.astype(BF16), (((0,), (0,)), ((), ())),
                           preferred_element_type=F32)


def _rms(x):
    return x * lax.rsqrt(jnp.mean(x * x, axis=-1, keepdims=True) + EPS)


def _sigmoid(x):
    return 1.0 / (1.0 + jnp.exp(-x))


def _row_iota(shape):
    return lax.broadcasted_iota(jnp.int32, shape, 0)


def _shift_down(x, fill_rows):
    s = len(fill_rows)
    y = pltpu.roll(x, s, 0)
    head = y[0:SUBLANES]
    row = _row_iota(head.shape)
    for j, r in enumerate(fill_rows):
        head = jnp.where(row == j, r, head)
    return jnp.concatenate([head, y[SUBLANES:]], axis=0)


def _shift_up(x, next_row):
    n = x.shape[0]
    y = pltpu.roll(x, n - 1, 0)
    tail = y[n - SUBLANES:n]
    tail = jnp.where(_row_iota(tail.shape) == SUBLANES - 1, next_row, tail)
    return jnp.concatenate([y[0:n - SUBLANES], tail], axis=0)


def _ada_kernel(c_ref, w_ref, b_ref, o_ref):
    c = c_ref[...]
    s = c * _sigmoid(c)
    o_ref[0] = _dot_f32(s, w_ref[0]) + b_ref[0]


def _ada(c_all, ada_w, ada_b):
    rows = c_all.shape[0]
    n_out = ada_w.shape[-1]
    tn = n_out // 4
    return pl.pallas_call(
        _ada_kernel,
        grid=(DEPTH, n_out // tn),
        in_specs=[pl.BlockSpec((rows, D_MODEL), lambda l, j: (0, 0)),
                  pl.BlockSpec((1, D_MODEL, tn), lambda l, j: (l, 0, j)),
                  pl.BlockSpec((1, 1, tn), lambda l, j: (l, 0, j))],
        out_specs=pl.BlockSpec((1, rows, tn), lambda l, j: (l, 0, j)),
        out_shape=jax.ShapeDtypeStruct((DEPTH, rows, n_out), F32),
        compiler_params=_params(2),
        name="ada",
    )(c_all, ada_w, ada_b.reshape(DEPTH, 1, n_out))


def _mod_spec(j, tiles_per_batch):
    return pl.BlockSpec((None, None, 1, D_MODEL), lambda i: (i // tiles_per_batch, j, 0, 0))


def _inproj_kernel(x_ref, nw_ref, sh_ref, sc_ref, w_ref, ohy_ref, ohg_ref, org_ref):
    h = _rms(x_ref[...]) * nw_ref[...]
    h = (h * (1.0 + sc_ref[...]) + sh_ref[...]).astype(BF16)
    ohy_ref[...] = jnp.dot(h, w_ref[:, 0:HY_IN], preferred_element_type=F32)
    ohg_ref[...] = jnp.dot(h, w_ref[:, HY_IN:HY_IN + HG_IN], preferred_element_type=F32)
    org_ref[...] = jnp.dot(h, w_ref[:, HY_IN + HG_IN:D_IN], preferred_element_type=F32)


def _inproj(x2, mod4, norm_w, w_in_bf16, seq_len):
    t = x2.shape[0]
    tm = min(TOKEN_TILE, seq_len)
    tpb = seq_len // tm
    row = lambda i: (i, 0)
    const = lambda i: (0, 0)
    return pl.pallas_call(
        _inproj_kernel,
        grid=(t // tm,),
        in_specs=[pl.BlockSpec((tm, D_MODEL), row),
                  pl.BlockSpec((1, D_MODEL), const),
                  _mod_spec(0, tpb), _mod_spec(1, tpb),
                  pl.BlockSpec((D_MODEL, D_IN), const)],
        out_specs=[pl.BlockSpec((tm, HY_IN), row), pl.BlockSpec((tm, HG_IN), row),
                   pl.BlockSpec((tm, RG_IN), row)],
        out_shape=[jax.ShapeDtypeStruct((t, HY_IN), F32), jax.ShapeDtypeStruct((t, HG_IN), F32),
                   jax.ShapeDtypeStruct((t, RG_IN), F32)],
        compiler_params=_params(1),
        name="inproj",
    )(x2, norm_w.reshape(1, D_MODEL), mod4, mod4, w_in_bf16)


def _seq_specs(tb, width, n_tiles, reverse=False):
    r8 = tb // SUBLANES
    last8 = n_tiles * r8 - 1

    def pos(i):
        return (n_tiles - 1 - i) if reverse else i

    main = pl.BlockSpec((None, tb, width), lambda b, i: (b, pos(i), 0))
    prev = pl.BlockSpec((None, SUBLANES, width),
                        lambda b, i: (b, jnp.maximum(pos(i) * r8 - 1, 0), 0))
    nxt = pl.BlockSpec((None, SUBLANES, width),
                       lambda b, i: (b, jnp.minimum((pos(i) + 1) * r8, last8), 0))
    return main, prev, nxt


def _hy_pre_kernel(u_ref, up_ref, un_ref, cw_ref, cb_ref, v_ref, x0_ref):
    i = pl.program_id(1)
    n = pl.num_programs(1)
    x = u_ref[...]
    prev = jnp.where(i > 0, up_ref[SUBLANES - 1:SUBLANES, :], 0.0)
    nxt = jnp.where(i < n - 1, un_ref[0:1, :], 0.0)
    uc = cb_ref[...] + _shift_down(x, [prev]) * cw_ref[0:1, :]
    uc = uc + x * cw_ref[1:2, :]
    uc = uc + _shift_up(x, nxt) * cw_ref[2:3, :]
    x0_ref[...] = uc[:, 0:HY_W]
    v_ref[...] = uc[:, 2 * HY_W:3 * HY_W] * uc[:, HY_W:2 * HY_W]


def _hy_pre(u_hy, conv_w, conv_b):
    b, l, _ = u_hy.shape
    tb = min(SEQ_TILE, l)
    nt = l // tb
    main, prev, nxt = _seq_specs(tb, HY_IN, nt)
    out = pl.BlockSpec((None, tb, HY_W), lambda bb, i: (bb, i, 0))
    return pl.pallas_call(
        _hy_pre_kernel,
        grid=(b, nt),
        in_specs=[main, prev, nxt,
                  pl.BlockSpec((3, HY_IN), lambda bb, i: (0, 0)),
                  pl.BlockSpec((1, HY_IN), lambda bb, i: (0, 0))],
        out_specs=[out, out],
        out_shape=[jax.ShapeDtypeStruct((b, l, HY_W), F32)] * 2,
        compiler_params=_params(2),
        name="hy_pre",
    )(u_hy, u_hy, u_hy, conv_w, conv_b.reshape(1, HY_IN))


def _hy_filter_kernel(fb_ref, w1t_ref, w1f_ref, b1_ref, w2_ref, b2_ref, w3_ref, b3_ref, w4_ref,
                      fr_ref, dl_ref, k_ref, *, seq_len, tile):
    half = tile // 2
    base = pl.program_id(0) * tile
    lane = lax.broadcasted_iota(jnp.int32, (half, LANES), 1)
    n = (_row_iota((half, LANES)) + base + jnp.where(lane >= HY_FFN, half, 0)).astype(F32)
    t = n / float(seq_len - 1)
    ang = fb_ref[...] * n * (2.0 * math.pi / seq_len)
    feat = jnp.where((lane & (HY_FFN - 1)) < HY_BANDS, jnp.cos(ang), -jnp.sin(ang))
    fr = fr_ref[...]
    h = jnp.sin(fr * (t * w1t_ref[...] + _dot_f32(feat, w1f_ref[...]) + b1_ref[...]))
    h = jnp.sin(fr * (_dot_f32(h, w2_ref[...]) + b2_ref[...]))
    h = jnp.sin(fr * (_dot_f32(h, w3_ref[...]) + b3_ref[...]))
    k = jnp.concatenate([_dot_f32(h, w4_ref[0]), _dot_f32(h, w4_ref[1])], axis=0)
    n2 = (_row_iota((tile, HY_W)) + base)
    decay = jnp.exp(-(n2.astype(F32) / float(seq_len - 1)) * dl_ref[...])
    k_ref[0] = k[:, 0:HY_W] * decay
    k_ref[1] = jnp.where(n2 == 0, 0.0, k[:, HY_W:2 * HY_W] * decay)


def _pad2(a, rows, cols):
    return jnp.pad(a, ((0, rows - a.shape[0]), (0, cols - a.shape[1])))


def _hy_filter(seq_len, w1, b1, w2, b2, w3, b3, w4, freq):
    tile = min(SEQ_TILE, seq_len)
    fb = jnp.linspace(1e-4, HY_BANDS - 1, HY_BANDS, dtype=F32)
    deltas = jnp.abs(jnp.linspace(math.log(HY_TARGET) / HY_SLOW_DECAY,
                                  math.log(HY_TARGET) / HY_FAST_DECAY, HY_W, dtype=F32))[None, :]
    zeros = jnp.zeros((HY_FFN, HY_FFN), F32)
    twice = lambda a: jnp.concatenate([a, a])[None, :]
    diag2 = lambda a: jnp.concatenate([jnp.concatenate([a, zeros], 1),
                                       jnp.concatenate([zeros, a], 1)], 0)
    w4z = jnp.zeros_like(w4)
    args = (twice(_pad2(jnp.concatenate([fb, fb])[None, :], 1, HY_FFN)[0]), twice(w1[0]),
            diag2(_pad2(w1[1:], HY_FFN, HY_FFN)), twice(b1), diag2(w2), twice(b2), diag2(w3), twice(b3),
            jnp.stack([jnp.concatenate([w4, w4z], 0), jnp.concatenate([w4z, w4], 0)]), twice(freq),
            deltas)
    const = lambda i: (0,) * 2
    return pl.pallas_call(
        functools.partial(_hy_filter_kernel, seq_len=seq_len, tile=tile),
        grid=(seq_len // tile,),
        in_specs=[pl.BlockSpec(a.shape, (lambda i: (0, 0, 0)) if a.ndim == 3 else const)
                  for a in args],
        out_specs=pl.BlockSpec((2, tile, HY_W), lambda i: (0, i, 0)),
        out_shape=jax.ShapeDtypeStruct((2, seq_len, HY_W), F32),
        compiler_params=_params(1),
        name="hy_filter",
    )(*args)


def _dft_tables(seq_len):
    n = 2 * seq_len
    n1 = DFT_N1
    n2 = n // n1
    grp = jnp.arange(n2 // SUBLANES, dtype=jnp.int32)[:, None, None, None]
    k1 = jnp.arange(n1, dtype=jnp.int32)[None, :, None, None]
    g = jnp.arange(SUBLANES, dtype=jnp.int32)[None, None, :, None]
    t1 = jnp.arange(n1 // 2, dtype=jnp.int32)[None, None, None, :]
    th = ((k1 * (n2 * t1 + SUBLANES * grp + g)) % n).astype(F32) * (2.0 * math.pi / n)
    trig = jnp.concatenate([jnp.cos(th), -jnp.sin(th)], axis=1)
    eye = jnp.eye(SUBLANES, dtype=F32)

    def expand_fwd(t):
        e = t[:, :, :, :, None] * eye[None, None, :, None, :].astype(t.dtype)
        return e.reshape(n2 // SUBLANES, 2 * n1 * SUBLANES, (n1 // 2) * SUBLANES)

    def expand_inv(t):
        e = jnp.transpose(t, (0, 3, 2, 1))[:, :, :, :, None] * eye[None, None, :, None, :].astype(t.dtype)
        return e.reshape(n2 // SUBLANES, (n1 // 2) * SUBLANES, 2 * n1 * SUBLANES)

    hi, lo = _split(trig)
    a = jnp.arange(n2, dtype=jnp.int32)
    ph = ((a[:, None] * a[None, :]) % n2).astype(F32) * (2.0 * math.pi / n2)
    c, s = jnp.cos(ph), jnp.sin(ph)
    g_fwd = jnp.concatenate([jnp.concatenate([c, s], 1), jnp.concatenate([-s, c], 1)], 0)
    g_inv = jnp.concatenate([jnp.concatenate([c, -s], 1), jnp.concatenate([s, c], 1)], 0)
    return dict(fwd=(expand_fwd(hi), expand_fwd(lo)), inv=expand_inv((trig * (1.0 / n)).astype(BF16)),
                g_fwd=_split(g_fwd), g_inv=g_inv.astype(BF16))


def _split(x):
    hi = x.astype(BF16)
    return hi, (x - hi.astype(F32)).astype(BF16)


def _dot_table(t_hi, t_lo, x, precise):
    if not precise:
        return jnp.dot(t_hi, x.astype(BF16), preferred_element_type=F32)
    x_hi, x_lo = _split(x)
    return (jnp.dot(t_hi, x_hi, preferred_element_type=F32)
            + jnp.dot(t_lo, x_hi, preferred_element_type=F32)
            + jnp.dot(t_hi, x_lo, preferred_element_type=F32))


def _dft_a_kernel(x_ref, eh_ref, *rest, precise):
    el_ref, y_ref = rest if precise else (None, rest[0])
    half, grp, width = x_ref.shape
    x = x_ref[...].reshape(half * grp, width)
    y = _dot_table(eh_ref[...], el_ref[...] if precise else None, x, precise)
    y_ref[...] = y.reshape(y_ref.shape)


def _dft_a(x, table, precise):
    nb, half, n2, _ = x.shape
    grp = SUBLANES
    tab = pl.BlockSpec((None, 4 * half * grp, half * grp), lambda j, b: (j, 0, 0))
    return pl.pallas_call(
        functools.partial(_dft_a_kernel, precise=precise),
        grid=(n2 // grp, nb),
        in_specs=[pl.BlockSpec((None, half, grp, HY_W), lambda j, b: (b, 0, j, 0))]
        + [tab] * (2 if precise else 1),
        out_specs=pl.BlockSpec((None, 4 * half, grp, HY_W), lambda j, b: (b, 0, j, 0)),
        out_shape=jax.ShapeDtypeStruct((nb, 4 * half, n2, HY_W), F32),
        compiler_params=_params(2),
        name="hy_dft_a",
    )(x, *table[:2 if precise else 1])


def _dft_b_filter_kernel(y_ref, gh_ref, gl_ref, k_ref, *, groups):
    gh, gl = gh_ref[...], gl_ref[...]
    n2 = y_ref.shape[-2]
    for j in range(groups):
        xf = _dot_table(gh, gl, jnp.concatenate([y_ref[0, 0, j], y_ref[0, 1, j]], axis=0), True)
        xb = _dot_table(gh, gl, jnp.concatenate([y_ref[1, 0, j], y_ref[1, 1, j]], axis=0), True)
        k_ref[0, j] = xf[0:n2] + xb[0:n2]
        k_ref[1, j] = xf[n2:2 * n2] - xb[n2:2 * n2]


def _dft_b_filter(y, g_fwd):
    _, _, n1, n2, _ = y.shape
    groups = max(1, DFT_ROWS // n2)
    const = lambda i: (0, 0)
    return pl.pallas_call(
        functools.partial(_dft_b_filter_kernel, groups=groups),
        grid=(n1 // groups,),
        in_specs=[pl.BlockSpec((2, 2, groups, n2, HY_W), lambda i: (0, 0, i, 0, 0)),
                  pl.BlockSpec((2 * n2, 2 * n2), const), pl.BlockSpec((2 * n2, 2 * n2), const)],
        out_specs=pl.BlockSpec((2, groups, n2, HY_W), lambda i: (0, i, 0, 0)),
        out_shape=jax.ShapeDtypeStruct((2, n1, n2, HY_W), F32),
        compiler_params=_params(1),
        name="hy_dft_b_filter",
    )(y, g_fwd[0], g_fwd[1])


def _dft_b_kernel(y_ref, k_ref, gf_ref, gi_ref, u_ref, *, groups):
    gf = gf_ref[...]
    gi = gi_ref[...]
    n2 = y_ref.shape[-2]
    for j in range(groups):
        x = _dot_table(gf, None, jnp.concatenate([y_ref[0, j], y_ref[1, j]], axis=0), False)
        xr, xi = x[0:n2], x[n2:2 * n2]
        kr, ki = k_ref[0, j], k_ref[1, j]
        z = jnp.concatenate([xr * kr - xi * ki, xr * ki + xi * kr], axis=0)
        u = _dot_table(gi, None, z, False)
        u_ref[0, j] = u[0:n2]
        u_ref[1, j] = u[n2:2 * n2]


def _dft_b(y, kspec, g_fwd, g_inv):
    nb, _, n1, n2, _ = y.shape
    groups = max(1, DFT_ROWS // n2)
    const = lambda i, b: (0, 0)
    return pl.pallas_call(
        functools.partial(_dft_b_kernel, groups=groups),
        grid=(n1 // groups, nb),
        in_specs=[pl.BlockSpec((None, 2, groups, n2, HY_W), lambda i, b: (b, 0, i, 0, 0)),
                  pl.BlockSpec((2, groups, n2, HY_W), lambda i, b: (0, i, 0, 0)),
                  pl.BlockSpec((2 * n2, 2 * n2), const), pl.BlockSpec((2 * n2, 2 * n2), const)],
        out_specs=pl.BlockSpec((None, 2, groups, n2, HY_W), lambda i, b: (b, 0, i, 0, 0)),
        out_shape=jax.ShapeDtypeStruct(y.shape, F32),
        compiler_params=_params(2),
        name="hy_dft_b",
    )(y, kspec, g_fwd, g_inv)


def _dft_a_inv_kernel(u_ref, e_ref, v_ref, x0_ref, bias_ref, gain_ref, o_ref):
    half, grp, width = v_ref.shape
    rows = half * grp
    u = u_ref[...].reshape(4 * rows, width)
    y = _dot_table(e_ref[...], None, u, False)
    y = (y + v_ref[...].reshape(rows, width) * bias_ref[...]) * x0_ref[...].reshape(rows, width)
    o_ref[...] = (_rms(y) * gain_ref[...]).reshape(o_ref.shape)


def _dft_a_inv(u, table, v, x0, bias, gain):
    nb, half, n2, _ = v.shape
    grp = SUBLANES
    tile = pl.BlockSpec((None, half, grp, HY_W), lambda j, b: (b, 0, j, 0))
    vec = pl.BlockSpec((1, HY_W), lambda j, b: (0, 0))
    return pl.pallas_call(
        _dft_a_inv_kernel,
        grid=(n2 // grp, nb),
        in_specs=[pl.BlockSpec((None, 4 * half, grp, HY_W), lambda j, b: (b, 0, j, 0)),
                  pl.BlockSpec((None, half * grp, 4 * half * grp), lambda j, b: (j, 0, 0)),
                  tile, tile, vec, vec],
        out_specs=tile,
        out_shape=jax.ShapeDtypeStruct(v.shape, F32),
        compiler_params=_params(2),
        name="hy_dft_a_inv",
    )(u, table, v, x0, bias, gain)


def _hyena(u_hy, p, l, gain, tables):
    b, seq_len, _ = u_hy.shape
    n1 = DFT_N1
    n2 = 2 * seq_len // n1
    v, x0 = _hy_pre(u_hy, p['hy_conv_w'][l], p['hy_conv_b'][l])
    kk = _hy_filter(seq_len, p['hy_w1'][l], p['hy_b1'][l], p['hy_w2'][l], p['hy_b2'][l],
                    p['hy_w3'][l], p['hy_b3'][l], p['hy_w4'][l], p['hy_freq'][l])
    ky = _dft_a(kk.reshape(2, n1 // 2, n2, HY_W), tables['fwd'], True)
    kspec = _dft_b_filter(ky.reshape(2, 2, n1, n2, HY_W), tables['g_fwd'])
    vm = v.reshape(b, n1 // 2, n2, HY_W)
    y = _dft_a(vm, tables['fwd'], False).reshape(b, 2, n1, n2, HY_W)
    u = _dft_b(y, kspec, tables['g_fwd'][0], tables['g_inv']).reshape(b, 2 * n1, n2, HY_W)
    out = _dft_a_inv(u, tables['inv'], vm, x0.reshape(b, n1 // 2, n2, HY_W),
                     p['hy_bias'][l].reshape(1, HY_W), gain.reshape(1, HY_W))
    return out.reshape(b * seq_len, HY_W)


def _cumsum_rows(g):
    n = g.shape[0]
    row = _row_iota((SUBLANES, g.shape[1]))
    s = 1
    while s < n:
        if s < SUBLANES:
            y = pltpu.roll(g, s, 0)
            head = jnp.where(row >= s, y[0:SUBLANES], 0.0)
            g = g + jnp.concatenate([head, y[SUBLANES:]], axis=0)
        else:
            g = jnp.concatenate([g[0:s], g[s:] + g[0:n - s]], axis=0)
        s *= 2
    return g


def _hg_chunk(q, v, z, lb, st, reverse):
    c = q.shape[0]
    e = jnp.exp2(z * (-LOG2E))
    s = 1.0 / (1.0 + e)
    g = jnp.log2(lb + (1.0 - lb) * s)
    kk = (1.0 - lb) * (e * s)
    bq = _cumsum_rows(g)
    total = bq[c - 1:c, :]
    if reverse:
        bq = total - bq + g
    blocks = []
    for i in range(c // HG_SUB):
        lo, hi = i * HG_SUB, (i + 1) * HG_SUB
        ks, ke = (lo, c) if reverse else (0, hi)
        ref = bq[lo + HG_SUB // 2:lo + HG_SUB // 2 + 1, :]
        qt = q[lo:hi] * jnp.exp2(bq[lo:hi] - ref)
        kt = kk[ks:ke] * jnp.exp2(ref - bq[ks:ke])
        parts = ([jnp.zeros((ks, HG_D), F32)] if ks else []) + [kt]
        parts += [jnp.zeros((c - ke, HG_D), F32)] if ke < c else []
        blocks.append(_dot_nt(qt, jnp.concatenate(parts, axis=0)))
    a = jnp.concatenate(blocks, axis=0)
    r2 = _row_iota((c, c))
    c2 = lax.broadcasted_iota(jnp.int32, (c, c), 1)
    a = jnp.where((c2 >= r2) if reverse else (c2 <= r2), a, 0.0)
    o = _dot_nt(q * jnp.exp2(bq), st) + _dot_bf16(a, v)
    st_new = st * jnp.exp2(total) + _dot_tn(v, kk * jnp.exp2(total - bq))
    return o, st_new


def _hg_kernel(*refs, reverse, combine, tb):
    if combine:
        q_ref, v_ref, z_ref, lb_ref, of_ref, g_ref, gain_ref, o_ref, st_ref, ob_ref = refs
    else:
        q_ref, v_ref, z_ref, lb_ref, o_ref, st_ref = refs
        ob_ref = o_ref

    @pl.when(pl.program_id(1) == 0)
    def _():
        st_ref[...] = jnp.zeros(st_ref.shape, F32)

    n_chunks = tb // HG_CHUNK

    def body(ci, carry):
        c = (n_chunks - 1 - ci) if reverse else ci
        r0 = pl.multiple_of(c * HG_CHUNK, HG_CHUNK)
        rows = pl.ds(r0, HG_CHUNK)
        for h in range(HG_HEADS):
            cols = slice(h * HG_D, (h + 1) * HG_D)
            o, st = _hg_chunk(q_ref[rows, cols], v_ref[rows, cols], z_ref[rows, cols],
                              lb_ref[:, cols], st_ref[h], reverse)
            st_ref[h] = st
            ob_ref[rows, cols] = o
        return carry

    lax.fori_loop(0, n_chunks, body, 0, unroll=HG_UNROLL)

    if combine:
        gate = g_ref[...]
        gate = gate * _sigmoid(gate)
        for h in range(HG_HEADS):
            cols = slice(h * HG_D, (h + 1) * HG_D)
            o = of_ref[:, cols] + ob_ref[:, cols]
            o_ref[:, cols] = _rms(o) * gain_ref[:, cols] * gate[:, cols]


def _hg_direction(u_hg, lb, reverse, o_fwd=None, gain=None):
    b, l, _ = u_hg.shape
    tb = min(SEQ_TILE, l)
    nt = l // tb
    combine = o_fwd is not None

    def col(j):
        return pl.BlockSpec((None, tb, HG_W),
                            lambda bb, i: (bb, (nt - 1 - i) if reverse else i, j))

    vec = pl.BlockSpec((1, HG_W), lambda bb, i: (0, 0))
    in_specs = [col(0), col(1), col(3 if reverse else 2), vec]
    args = [u_hg, u_hg, u_hg, lb.reshape(1, HG_W)]
    scratch = [pltpu.VMEM((HG_HEADS, HG_D, HG_D), F32)]
    if combine:
        in_specs += [col(0), col(4), vec]
        args += [o_fwd, u_hg, gain.reshape(1, HG_W)]
        scratch += [pltpu.VMEM((tb, HG_W), F32)]
    return pl.pallas_call(
        functools.partial(_hg_kernel, reverse=reverse, combine=combine, tb=tb),
        grid=(b, nt),
        in_specs=in_specs,
        out_specs=col(0),
        out_shape=jax.ShapeDtypeStruct((b, l, HG_W), F32),
        scratch_shapes=scratch,
        compiler_params=_params(2),
        name="hg_bwd" if reverse else "hg_fwd",
    )(*args)


def _hgrn2(u_hg, lb_f, lb_b, gain):
    b, l, _ = u_hg.shape
    o_f = _hg_direction(u_hg, lb_f, reverse=False)
    y = _hg_direction(u_hg, lb_b, reverse=True, o_fwd=o_f, gain=gain)
    return y.reshape(b * l, HG_W)


def _scan8(a, b, reverse):
    row = _row_iota(a.shape)
    for s in (1, 2, 4):
        if reverse:
            keep = row < SUBLANES - s
            a_s, b_s = pltpu.roll(a, SUBLANES - s, 0), pltpu.roll(b, SUBLANES - s, 0)
        else:
            keep = row >= s
            a_s, b_s = pltpu.roll(a, s, 0), pltpu.roll(b, s, 0)
        b = jnp.where(keep, a * b_s, 0.0) + b
        a = jnp.where(keep, a * a_s, a)
    return a, b


def _rg_kernel(*refs, reverse, combine, tb):
    if combine:
        (u_ref, up_ref, un_ref, cw_ref, cb_ref, wh_ref, wl_ref, gb_ref, lam_ref,
         hf_ref, gain_ref, o_ref, a_ref, b_ref, carry_ref) = refs
        h_ref = b_ref
    else:
        (u_ref, up_ref, un_ref, cw_ref, cb_ref, wh_ref, wl_ref, gb_ref, lam_ref,
         o_ref, a_ref, b_ref, carry_ref) = refs
        h_ref = o_ref
    step = pl.program_id(1)
    n = pl.num_programs(1)
    pos = (n - 1 - step) if reverse else step

    @pl.when(step == 0)
    def _():
        carry_ref[...] = jnp.zeros(carry_ref.shape, F32)

    x = u_ref[:, 0:RG_W]
    has_prev = pos > 0
    has_next = pos < n - 1
    p6 = jnp.where(has_prev, up_ref[SUBLANES - 2:SUBLANES - 1, 0:RG_W], 0.0)
    p7 = jnp.where(has_prev, up_ref[SUBLANES - 1:SUBLANES, 0:RG_W], 0.0)
    nx = jnp.where(has_next, un_ref[0:1, 0:RG_W], 0.0)
    xr = cb_ref[...] + _shift_down(x, [p6, p7]) * cw_ref[0:1, :]
    xr = xr + _shift_down(x, [p7]) * cw_ref[1:2, :]
    xr = xr + x * cw_ref[2:3, :]
    xr = xr + _shift_up(x, nx) * cw_ref[3:4, :]

    x_hi = xr.astype(BF16)
    x_lo = (xr - x_hi.astype(F32)).astype(BF16)
    wh = wh_ref[...]
    pre = (jnp.dot(x_hi, wh, preferred_element_type=F32)
           + jnp.dot(x_lo, wh, preferred_element_type=F32)
           + jnp.dot(x_hi, wl_ref[...], preferred_element_type=F32)) + gb_ref[...]
    r = _sigmoid(pre[:, 0:RG_W])
    ig = _sigmoid(pre[:, RG_W:2 * RG_W])
    nl = -lam_ref[...]
    softplus = jnp.maximum(nl, 0.0) + jnp.log1p(jnp.exp(-jnp.abs(nl)))
    a = jnp.exp(-RG_C * r * softplus)
    a_ref[...] = a
    b_ref[...] = jnp.sqrt(1.0 - a * a) * (ig * xr)

    n_groups = tb // SUBLANES

    def body(gi, carry):
        g = (n_groups - 1 - gi) if reverse else gi
        rows = pl.ds(pl.multiple_of(g * SUBLANES, SUBLANES), SUBLANES)
        a, b = _scan8(a_ref[rows, :], b_ref[rows, :], reverse)
        h = a * carry + b
        h_ref[rows, :] = h
        return h[0:1, :] if reverse else h[SUBLANES - 1:SUBLANES, :]

    carry_ref[...] = lax.fori_loop(0, n_groups, body, carry_ref[...], unroll=RG_UNROLL)

    if combine:
        gate = u_ref[:, RG_W:2 * RG_W]
        gelu = 0.5 * gate * (1.0 + jnp.tanh(math.sqrt(2.0 / math.pi)
                                             * (gate + 0.044715 * (gate * gate * gate))))
        y = (hf_ref[...] + h_ref[...]) * gelu
        o_ref[...] = _rms(y) * gain_ref[...]


def _blockdiag(w):
    h, d, _ = w.shape
    eye = jnp.eye(h, dtype=w.dtype)
    return (eye[:, None, :, None] * w[:, :, None, :]).reshape(h * d, h * d)


def _rg_direction(u_rg, p, l, d, h_fwd=None, gain=None):
    b, seq_len, _ = u_rg.shape
    tb = min(SEQ_TILE, seq_len)
    nt = seq_len // tb
    reverse = d == 1
    combine = h_fwd is not None
    main, prev, nxt = _seq_specs(tb, RG_IN, nt, reverse)
    w = jnp.concatenate([_blockdiag(p['rg_wa'][l, d]), _blockdiag(p['rg_wx'][l, d])], axis=1)
    w_hi = w.astype(BF16)
    w_lo = (w - w_hi.astype(F32)).astype(BF16)
    gb = jnp.concatenate([p['rg_ba'][l, d], p['rg_bx'][l, d]]).reshape(1, 2 * RG_W)
    const = lambda bb, i: (0, 0)
    vec = pl.BlockSpec((1, RG_W), const)
    out = pl.BlockSpec((None, tb, RG_W), lambda bb, i: (bb, (nt - 1 - i) if reverse else i, 0))
    in_specs = [main, prev, nxt, pl.BlockSpec((4, RG_W), const), vec,
                pl.BlockSpec((RG_W, 2 * RG_W), const), pl.BlockSpec((RG_W, 2 * RG_W), const),
                pl.BlockSpec((1, 2 * RG_W), const), vec]
    args = [u_rg, u_rg, u_rg, p['rg_conv_w'][l], p['rg_conv_b'][l].reshape(1, RG_W), w_hi, w_lo, gb,
            p['rg_lambda'][l, d].reshape(1, RG_W)]
    if combine:
        in_specs += [out, vec]
        args += [h_fwd, gain.reshape(1, RG_W)]
    return pl.pallas_call(
        functools.partial(_rg_kernel, reverse=reverse, combine=combine, tb=tb),
        grid=(b, nt),
        in_specs=in_specs,
        out_specs=out,
        out_shape=jax.ShapeDtypeStruct((b, seq_len, RG_W), F32),
        scratch_shapes=[pltpu.VMEM((tb, RG_W), F32), pltpu.VMEM((tb, RG_W), F32),
                        pltpu.VMEM((1, RG_W), F32)],
        compiler_params=_params(2),
        name="rg_bwd" if reverse else "rg_fwd",
    )(*args)


def _rglru(u_rg, p, l, gain):
    b, seq_len, _ = u_rg.shape
    h_f = _rg_direction(u_rg, p, l, 0)
    y = _rg_direction(u_rg, p, l, 1, h_fwd=h_f, gain=gain)
    return y.reshape(b * seq_len, RG_W)


def _outffn_kernel(*refs, final, ff_chunks):
    if final:
        (x_ref, yhy_ref, yhg_ref, yrg_ref, g1_ref, sh2_ref, sc2_ref, g2_ref, n2_ref,
         wo_ref, wg_ref, wu_ref, wd_ref, fn_ref, o_ref) = refs
    else:
        (x_ref, yhy_ref, yhg_ref, yrg_ref, g1_ref, sh2_ref, sc2_ref, g2_ref, n2_ref,
         wo_ref, wg_ref, wu_ref, wd_ref, o_ref) = refs
    mix = jnp.dot(yhy_ref[...].astype(BF16), wo_ref[0:HY_W, :], preferred_element_type=F32)
    mix = mix + jnp.dot(yhg_ref[...].astype(BF16), wo_ref[HY_W:HY_W + HG_W, :],
                        preferred_element_type=F32)
    mix = mix + jnp.dot(yrg_ref[...].astype(BF16), wo_ref[HY_W + HG_W:D_MODEL, :],
                        preferred_element_type=F32)
    x1 = x_ref[...] + g1_ref[...] * mix
    h = _rms(x1) * n2_ref[...]
    h = (h * (1.0 + sc2_ref[...]) + sh2_ref[...]).astype(BF16)
    step = D_FF // ff_chunks
    ff = None
    for c in range(ff_chunks):
        sl = slice(c * step, (c + 1) * step)
        gate = jnp.dot(h, wg_ref[:, sl], preferred_element_type=F32)
        up = jnp.dot(h, wu_ref[:, sl], preferred_element_type=F32)
        act = ((gate * _sigmoid(gate)) * up).astype(BF16)
        part = jnp.dot(act, wd_ref[sl, :], preferred_element_type=F32)
        ff = part if ff is None else ff + part
    x2 = x1 + g2_ref[...] * ff
    if final:
        x2 = _rms(x2) * fn_ref[...]
    o_ref[...] = x2


def _outffn(x2, y_hy, y_hg, y_rg, mod4, norm_w, w_out, wg, wu, wd, seq_len, final_w=None):
    t = x2.shape[0]
    tm = min(TOKEN_TILE, seq_len)
    tpb = seq_len // tm
    final = final_w is not None
    row = lambda i: (i, 0)
    const = lambda i: (0, 0)

    def resident(shape):
        return pl.BlockSpec(shape, const, pipeline_mode=pl.Buffered(1))

    in_specs = [pl.BlockSpec((tm, D_MODEL), row), pl.BlockSpec((tm, HY_W), row),
                pl.BlockSpec((tm, HG_W), row), pl.BlockSpec((tm, RG_W), row),
                _mod_spec(2, tpb), _mod_spec(3, tpb), _mod_spec(4, tpb), _mod_spec(5, tpb),
                pl.BlockSpec((1, D_MODEL), const),
                resident((D_MODEL, D_MODEL)), resident((D_MODEL, D_FF)), resident((D_MODEL, D_FF)),
                resident((D_FF, D_MODEL))]
    args = [x2, y_hy, y_hg, y_rg, mod4, mod4, mod4, mod4, norm_w.reshape(1, D_MODEL),
            w_out, wg, wu, wd]
    if final:
        in_specs.append(pl.BlockSpec((1, D_MODEL), const))
        args.append(final_w.reshape(1, D_MODEL))
    return pl.pallas_call(
        functools.partial(_outffn_kernel, final=final, ff_chunks=2),
        grid=(t // tm,),
        in_specs=in_specs,
        out_specs=pl.BlockSpec((tm, D_MODEL), row),
        out_shape=jax.ShapeDtypeStruct((t, D_MODEL), F32),
        compiler_params=_params(1),
        name="outffn",
    )(*args)


def _trunk(x, mods, lb_all, p, wb):
    b, seq_len, _ = x.shape
    x2 = x.reshape(b * seq_len, D_MODEL)
    tables = _dft_tables(seq_len)
    for l in range(DEPTH):
        mod4 = mods[l].reshape(b, 6, 1, D_MODEL)
        gain = p['out_norm_w'][l]
        u_hy, u_hg, u_rg = _inproj(x2, mod4, p['norm1_w'][l], wb['w_in'][l], seq_len)
        y_hy = _hyena(u_hy.reshape(b, seq_len, HY_IN), p, l, gain[0:HY_W], tables)
        y_hg = _hgrn2(u_hg.reshape(b, seq_len, HG_IN), lb_all[0, l], lb_all[1, l],
                      gain[HY_W:HY_W + HG_W])
        y_rg = _rglru(u_rg.reshape(b, seq_len, RG_IN), p, l, gain[HY_W + HG_W:])
        x2 = _outffn(x2, y_hy, y_hg, y_rg, mod4, p['norm2_w'][l], wb['w_out'][l], wb['ffn_wg'][l],
                     wb['ffn_wu'][l], wb['ffn_wd'][l], seq_len,
                     final_w=p['final_norm_w'] if l == DEPTH - 1 else None)
    return x2.reshape(b, seq_len, D_MODEL)


def kernel(x_prompt, x_sample, c_prompt, c_sample, w_in, w_out, out_norm_w, ada_w, ada_b, norm1_w, norm2_w, final_norm_w, hy_conv_w, hy_conv_b, hy_w1, hy_b1, hy_w2, hy_b2, hy_w3, hy_b3, hy_w4, hy_freq, hy_bias, hg_lb_logits, rg_conv_w, rg_conv_b, rg_wa, rg_ba, rg_wx, rg_bx, rg_lambda, ffn_wg, ffn_wu, ffn_wd):
    p = dict(out_norm_w=out_norm_w, norm1_w=norm1_w, norm2_w=norm2_w, final_norm_w=final_norm_w,
             hy_conv_w=hy_conv_w, hy_conv_b=hy_conv_b, hy_w1=hy_w1, hy_b1=hy_b1, hy_w2=hy_w2,
             hy_b2=hy_b2, hy_w3=hy_w3, hy_b3=hy_b3, hy_w4=hy_w4, hy_freq=hy_freq, hy_bias=hy_bias,
             rg_conv_w=rg_conv_w, rg_conv_b=rg_conv_b, rg_wa=rg_wa, rg_ba=rg_ba, rg_wx=rg_wx,
             rg_bx=rg_bx, rg_lambda=rg_lambda)
    wb = dict(w_in=w_in.astype(BF16), w_out=w_out.astype(BF16), ffn_wg=ffn_wg.astype(BF16),
              ffn_wu=ffn_wu.astype(BF16), ffn_wd=ffn_wd.astype(BF16))
    lb_all = jax.nn.softmax(hg_lb_logits.astype(F32), axis=1)
    lb_all = jnp.cumsum(lb_all, axis=1) - lb_all[:, :1]
    nb_p, nb_s = c_prompt.shape[0], c_sample.shape[0]
    rows = -(-(nb_p + nb_s) // SUBLANES) * SUBLANES
    c_all = jnp.pad(jnp.concatenate([c_prompt, c_sample]), ((0, rows - nb_p - nb_s), (0, 0)))
    mods = _ada(c_all, ada_w, ada_b)
    y_prompt = _trunk(x_prompt, mods[:, 0:nb_p], lb_all, p, wb)
    y_sample = _trunk(x_sample, mods[:, nb_p:nb_p + nb_s], lb_all, p, wb)
    return (y_prompt, y_sample)
```

```python
import functools
import math

import jax
import jax.numpy as jnp
from jax import lax
from jax.experimental import pallas as pl
from jax.experimental.pallas import tpu as pltpu

F32 = jnp.float32
BF16 = jnp.bfloat16
HIGHEST = lax.Precision.HIGHEST

D_MODEL = 1024
DEPTH = 2
HY_W = 256
HG_W = 512
RG_W = 256
HY_IN = 3 * HY_W
HG_IN = 5 * HG_W
RG_IN = 2 * RG_W
D_IN = HY_IN + HG_IN + RG_IN
HY_EMB = 33
HY_BANDS = 16
HY_FFN = 64
HY_FAST_DECAY = 0.3
HY_SLOW_DECAY = 1.5
HY_TARGET = 1e-2
HG_HEADS = 4
HG_D = 128
RG_HEADS = 4
RG_HD = 64
RG_C = 8.0
D_FF = 2816
EPS = 1e-6
LOG2E = 1.4426950408889634

LANES = 128
SUBLANES = 8
VMEM_LIMIT = 56 * 1024 * 1024

TOKEN_TILE = 512
SEQ_TILE = 512
HG_CHUNK = 64
HG_SUB = 32
HG_UNROLL = 8
RG_UNROLL = 8
DFT_N1 = 128
DFT_ROWS = 2048


def _params(n_axes):
    return pltpu.CompilerParams(dimension_semantics=("arbitrary",) * n_axes,
                                vmem_limit_bytes=VMEM_LIMIT)


def _dot_bf16(a, b):
    return jnp.dot(a.astype(BF16), b.astype(BF16), preferred_element_type=F32)


def _dot_f32(a, b):
    return jnp.dot(a, b, preferred_element_type=F32, precision=HIGHEST)


def _dot_nt(a, b):
    return lax.dot_general(a.astype(BF16), b.astype(BF16), (((1,), (1,)), ((), ())),
                           preferred_element_type=F32)


def _dot_tn(a, b):
    return lax.dot_general(a.astype(BF16), b.astype(BF16), (((0,), (0,)), ((), ())),
                           preferred_element_type=F32)


def _rms(x):
    return x * lax.rsqrt(jnp.mean(x * x, axis=-1, keepdims=True) + EPS)


def _sigmoid(x):
    return 1.0 / (1.0 + jnp.exp(-x))


def _row_iota(shape):
    return lax.broadcasted_iota(jnp.int32, shape, 0)


def _shift_down(x, fill_rows):
    s = len(fill_rows)
    y = pltpu.roll(x, s, 0)
    head = y[0:SUBLANES]
    row = _row_iota(head.shape)
    for j, r in enumerate(fill_rows):
        head = jnp.where(row == j, r, head)
    return jnp.concatenate([head, y[SUBLANES:]], axis=0)


def _shift_up(x, next_row):
    n = x.shape[0]
    y = pltpu.roll(x, n - 1, 0)
    tail = y[n - SUBLANES:n]
    tail = jnp.where(_row_iota(tail.shape) == SUBLANES - 1, next_row, tail)
    return jnp.concatenate([y[0:n - SUBLANES], tail], axis=0)


def _ada_kernel(c_ref, w_ref, b_ref, o_ref):
    c = c_ref[...]
    s = c * _sigmoid(c)
    o_ref[0] = _dot_f32(s, w_ref[0]) + b_ref[0]


def _ada(c_all, ada_w, ada_b):
    rows = c_all.shape[0]
    n_out = ada_w.shape[-1]
    tn = n_out // 4
    return pl.pallas_call(
        _ada_kernel,
        grid=(DEPTH, n_out // tn),
        in_specs=[pl.BlockSpec((rows, D_MODEL), lambda l, j: (0, 0)),
                  pl.BlockSpec((1, D_MODEL, tn), lambda l, j: (l, 0, j)),
                  pl.BlockSpec((1, 1, tn), lambda l, j: (l, 0, j))],
        out_specs=pl.BlockSpec((1, rows, tn), lambda l, j: (l, 0, j)),
        out_shape=jax.ShapeDtypeStruct((DEPTH, rows, n_out), F32),
        compiler_params=_params(2),
        name="ada",
    )(c_all, ada_w, ada_b.reshape(DEPTH, 1, n_out))


def _mod_spec(j, tiles_per_batch):
    return pl.BlockSpec((None, None, 1, D_MODEL), lambda i: (i // tiles_per_batch, j, 0, 0))


def _inproj_kernel(x_ref, nw_ref, sh_ref, sc_ref, w_ref, ohy_ref, ohg_ref, org_ref):
    h = _rms(x_ref[...]) * nw_ref[...]
    h = (h * (1.0 + sc_ref[...]) + sh_ref[...]).astype(BF16)
    ohy_ref[...] = jnp.dot(h, w_ref[:, 0:HY_IN], preferred_element_type=F32)
    ohg_ref[...] = jnp.dot(h, w_ref[:, HY_IN:HY_IN + HG_IN], preferred_element_type=F32)
    org_ref[...] = jnp.dot(h, w_ref[:, HY_IN + HG_IN:D_IN], preferred_element_type=F32)


def _inproj(x2, mod4, norm_w, w_in_bf16, seq_len):
    t = x2.shape[0]
    tm = min(TOKEN_TILE, seq_len)
    tpb = seq_len // tm
    row = lambda i: (i, 0)
    const = lambda i: (0, 0)
    return pl.pallas_call(
        _inproj_kernel,
        grid=(t // tm,),
        in_specs=[pl.BlockSpec((tm, D_MODEL), row),
                  pl.BlockSpec((1, D_MODEL), const),
                  _mod_spec(0, tpb), _mod_spec(1, tpb),
                  pl.BlockSpec((D_MODEL, D_IN), const)],
        out_specs=[pl.BlockSpec((tm, HY_IN), row), pl.BlockSpec((tm, HG_IN), row),
                   pl.BlockSpec((tm, RG_IN), row)],
        out_shape=[jax.ShapeDtypeStruct((t, HY_IN), F32), jax.ShapeDtypeStruct((t, HG_IN), F32),
                   jax.ShapeDtypeStruct((t, RG_IN), F32)],
        compiler_params=_params(1),
        name="inproj",
    )(x2, norm_w.reshape(1, D_MODEL), mod4, mod4, w_in_bf16)


def _seq_specs(tb, width, n_tiles, reverse=False):
    r8 = tb // SUBLANES
    last8 = n_tiles * r8 - 1

    def pos(i):
        return (n_tiles - 1 - i) if reverse else i

    main = pl.BlockSpec((None, tb, width), lambda b, i: (b, pos(i), 0))
    prev = pl.BlockSpec((None, SUBLANES, width),
                        lambda b, i: (b, jnp.maximum(pos(i) * r8 - 1, 0), 0))
    nxt = pl.BlockSpec((None, SUBLANES, width),
                       lambda b, i: (b, jnp.minimum((pos(i) + 1) * r8, last8), 0))
    return main, prev, nxt


def _hy_pre_kernel(u_ref, up_ref, un_ref, cw_ref, cb_ref, v_ref, x0_ref):
    i = pl.program_id(1)
    n = pl.num_programs(1)
    x = u_ref[...]
    prev = jnp.where(i > 0, up_ref[SUBLANES - 1:SUBLANES, :], 0.0)
    nxt = jnp.where(i < n - 1, un_ref[0:1, :], 0.0)
    uc = cb_ref[...] + _shift_down(x, [prev]) * cw_ref[0:1, :]
    uc = uc + x * cw_ref[1:2, :]
    uc = uc + _shift_up(x, nxt) * cw_ref[2:3, :]
    x0_ref[...] = uc[:, 0:HY_W]
    v_ref[...] = uc[:, 2 * HY_W:3 * HY_W] * uc[:, HY_W:2 * HY_W]


def _hy_pre(u_hy, conv_w, conv_b):
    b, l, _ = u_hy.shape
    tb = min(SEQ_TILE, l)
    nt = l // tb
    main, prev, nxt = _seq_specs(tb, HY_IN, nt)
    out = pl.BlockSpec((None, tb, HY_W), lambda bb, i: (bb, i, 0))
    return pl.pallas_call(
        _hy_pre_kernel,
        grid=(b, nt),
        in_specs=[main, prev, nxt,
                  pl.BlockSpec((3, HY_IN), lambda bb, i: (0, 0)),
                  pl.BlockSpec((1, HY_IN), lambda bb, i: (0, 0))],
        out_specs=[out, out],
        out_shape=[jax.ShapeDtypeStruct((b, l, HY_W), F32)] * 2,
        compiler_params=_params(2),
        name="hy_pre",
    )(u_hy, u_hy, u_hy, conv_w, conv_b.reshape(1, HY_IN))


def _hy_filter_kernel(fb_ref, w1t_ref, w1f_ref, b1_ref, w2_ref, b2_ref, w3_ref, b3_ref, w4_ref,
                      fr_ref, dl_ref, k_ref, *, seq_len, tile):
    half = tile // 2
    base = pl.program_id(0) * tile
    lane = lax.broadcasted_iota(jnp.int32, (half, LANES), 1)
    n = (_row_iota((half, LANES)) + base + jnp.where(lane >= HY_FFN, half, 0)).astype(F32)
    t = n / float(seq_len - 1)
    ang = fb_ref[...] * n * (2.0 * math.pi / seq_len)
    feat = jnp.where((lane & (HY_FFN - 1)) < HY_BANDS, jnp.cos(ang), -jnp.sin(ang))
    fr = fr_ref[...]
    h = jnp.sin(fr * (t * w1t_ref[...] + _dot_f32(feat, w1f_ref[...]) + b1_ref[...]))
    h = jnp.sin(fr * (_dot_f32(h, w2_ref[...]) + b2_ref[...]))
    h = jnp.sin(fr * (_dot_f32(h, w3_ref[...]) + b3_ref[...]))
    k = jnp.concatenate([_dot_f32(h, w4_ref[0]), _dot_f32(h, w4_ref[1])], axis=0)
    n2 = (_row_iota((tile, HY_W)) + base)
    decay = jnp.exp(-(n2.astype(F32) / float(seq_len - 1)) * dl_ref[...])
    k_ref[0] = k[:, 0:HY_W] * decay
    k_ref[1] = jnp.where(n2 == 0, 0.0, k[:, HY_W:2 * HY_W] * decay)


def _pad2(a, rows, cols):
    return jnp.pad(a, ((0, rows - a.shape[0]), (0, cols - a.shape[1])))


def _hy_filter(seq_len, w1, b1, w2, b2, w3, b3, w4, freq):
    tile = min(SEQ_TILE, seq_len)
    fb = jnp.linspace(1e-4, HY_BANDS - 1, HY_BANDS, dtype=F32)
    deltas = jnp.abs(jnp.linspace(math.log(HY_TARGET) / HY_SLOW_DECAY,
                                  math.log(HY_TARGET) / HY_FAST_DECAY, HY_W, dtype=F32))[None, :]
    zeros = jnp.zeros((HY_FFN, HY_FFN), F32)
    twice = lambda a: jnp.concatenate([a, a])[None, :]
    diag2 = lambda a: jnp.concatenate([jnp.concatenate([a, zeros], 1),
                                       jnp.concatenate([zeros, a], 1)], 0)
    w4z = jnp.zeros_like(w4)
    args = (twice(_pad2(jnp.concatenate([fb, fb])[None, :], 1, HY_FFN)[0]), twice(w1[0]),
            diag2(_pad2(w1[1:], HY_FFN, HY_FFN)), twice(b1), diag2(w2), twice(b2), diag2(w3), twice(b3),
            jnp.stack([jnp.concatenate([w4, w4z], 0), jnp.concatenate([w4z, w4], 0)]), twice(freq),
            deltas)
    const = lambda i: (0,) * 2
    return pl.pallas_call(
        functools.partial(_hy_filter_kernel, seq_len=seq_len, tile=tile),
        grid=(seq_len // tile,),
        in_specs=[pl.BlockSpec(a.shape, (lambda i: (0, 0, 0)) if a.ndim == 3 else const)
                  for a in args],
        out_specs=pl.BlockSpec((2, tile, HY_W), lambda i: (0, i, 0)),
        out_shape=jax.ShapeDtypeStruct((2, seq_len, HY_W), F32),
        compiler_params=_params(1),
        name="hy_filter",
    )(*args)


def _dft_tables(seq_len):
    n = 2 * seq_len
    n1 = DFT_N1
    n2 = n // n1
    grp = jnp.arange(n2 // SUBLANES, dtype=jnp.int32)[:, None, None, None]
    k1 = jnp.arange(n1, dtype=jnp.int32)[None, :, None, None]
    g = jnp.arange(SUBLANES, dtype=jnp.int32)[None, None, :, None]
    t1 = jnp.arange(n1 // 2, dtype=jnp.int32)[None, None, None, :]
    th = ((k1 * (n2 * t1 + SUBLANES * grp + g)) % n).astype(F32) * (2.0 * math.pi / n)
    trig = jnp.concatenate([jnp.cos(th), -jnp.sin(th)], axis=1)
    groups = n2 // SUBLANES
    hi, lo = _split(trig)
    hi, lo = (t.reshape(groups, 2 * n1 * SUBLANES, n1 // 2) for t in (hi, lo))
    inv = jnp.transpose((trig * (1.0 / n)).astype(BF16), (0, 3, 2, 1))
    inv = inv.reshape(groups, (n1 // 2) * SUBLANES, 2 * n1)
    a = jnp.arange(n2, dtype=jnp.int32)
    ph = ((a[:, None] * a[None, :]) % n2).astype(F32) * (2.0 * math.pi / n2)
    c, s = jnp.cos(ph), jnp.sin(ph)
    g_fwd = jnp.concatenate([jnp.concatenate([c, s], 1), jnp.concatenate([-s, c], 1)], 0)
    g_inv = jnp.concatenate([jnp.concatenate([c, -s], 1), jnp.concatenate([s, c], 1)], 0)
    return dict(fwd=(hi, lo), inv=inv, g_fwd=_split(g_fwd), g_inv=g_inv.astype(BF16))


def _expand_cols(t):
    rows, k = t.shape
    wide = k * SUBLANES
    rep = (lax.broadcasted_iota(jnp.int32, (k, wide), 1) // SUBLANES
           == lax.broadcasted_iota(jnp.int32, (k, wide), 0))
    e = jnp.dot(t, jnp.where(rep, 1.0, 0.0).astype(BF16), preferred_element_type=F32)
    keep = ((_row_iota((rows, wide)) & (SUBLANES - 1))
            == (lax.broadcasted_iota(jnp.int32, (rows, wide), 1) & (SUBLANES - 1)))
    return jnp.where(keep, e, 0.0).astype(BF16)


def _split(x):
    hi = x.astype(BF16)
    return hi, (x - hi.astype(F32)).astype(BF16)


def _dot_table(t_hi, t_lo, x, precise):
    if not precise:
        return jnp.dot(t_hi, x.astype(BF16), preferred_element_type=F32)
    x_hi, x_lo = _split(x)
    return (jnp.dot(t_hi, x_hi, preferred_element_type=F32)
            + jnp.dot(t_lo, x_hi, preferred_element_type=F32)
            + jnp.dot(t_hi, x_lo, preferred_element_type=F32))


def _dft_a_kernel(*refs, precise):
    if precise:
        x_ref, th_ref, tl_ref, y_ref, eh_ref, el_ref = refs
    else:
        x_ref, th_ref, y_ref, eh_ref = refs

    @pl.when(pl.program_id(1) == 0)
    def _():
        eh_ref[...] = _expand_cols(th_ref[...])
        if precise:
            el_ref[...] = _expand_cols(tl_ref[...])

    half, grp, width = x_ref.shape
    x = x_ref[...].reshape(half * grp, width)
    y = _dot_table(eh_ref[...], el_ref[...] if precise else None, x, precise)
    y_ref[...] = y.reshape(y_ref.shape)


def _dft_a(x, table, precise):
    nb, half, n2, _ = x.shape
    grp = SUBLANES
    n_tab = 2 if precise else 1
    tab = pl.BlockSpec((None, 4 * half * grp, half), lambda j, b: (j, 0, 0))
    return pl.pallas_call(
        functools.partial(_dft_a_kernel, precise=precise),
        grid=(n2 // grp, nb),
        in_specs=[pl.BlockSpec((None, half, grp, HY_W), lambda j, b: (b, 0, j, 0))] + [tab] * n_tab,
        out_specs=pl.BlockSpec((None, 4 * half, grp, HY_W), lambda j, b: (b, 0, j, 0)),
        out_shape=jax.ShapeDtypeStruct((nb, 4 * half, n2, HY_W), F32),
        scratch_shapes=[pltpu.VMEM((4 * half * grp, half * grp), BF16)] * n_tab,
        compiler_params=_params(2),
        name="hy_dft_a",
    )(x, *table[:n_tab])


def _dft_b_filter_kernel(y_ref, gh_ref, gl_ref, k_ref, *, groups):
    gh, gl = gh_ref[...], gl_ref[...]
    n2 = y_ref.shape[-2]
    for j in range(groups):
        xf = _dot_table(gh, gl, jnp.concatenate([y_ref[0, 0, j], y_ref[0, 1, j]], axis=0), True)
        xb = _dot_table(gh, gl, jnp.concatenate([y_ref[1, 0, j], y_ref[1, 1, j]], axis=0), True)
        k_ref[0, j] = xf[0:n2] + xb[0:n2]
        k_ref[1, j] = xf[n2:2 * n2] - xb[n2:2 * n2]


def _dft_b_filter(y, g_fwd):
    _, _, n1, n2, _ = y.shape
    groups = max(1, DFT_ROWS // n2)
    const = lambda i: (0, 0)
    return pl.pallas_call(
        functools.partial(_dft_b_filter_kernel, groups=groups),
        grid=(n1 // groups,),
        in_specs=[pl.BlockSpec((2, 2, groups, n2, HY_W), lambda i: (0, 0, i, 0, 0)),
                  pl.BlockSpec((2 * n2, 2 * n2), const), pl.BlockSpec((2 * n2, 2 * n2), const)],
        out_specs=pl.BlockSpec((2, groups, n2, HY_W), lambda i: (0, i, 0, 0)),
        out_shape=jax.ShapeDtypeStruct((2, n1, n2, HY_W), F32),
        compiler_params=_params(1),
        name="hy_dft_b_filter",
    )(y, g_fwd[0], g_fwd[1])


def _dft_b_kernel(y_ref, k_ref, gf_ref, gi_ref, u_ref, *, groups):
    gf = gf_ref[...]
    gi = gi_ref[...]
    n2 = y_ref.shape[-2]
    for j in range(groups):
        x = _dot_table(gf, None, jnp.concatenate([y_ref[0, j], y_ref[1, j]], axis=0), False)
        xr, xi = x[0:n2], x[n2:2 * n2]
        kr, ki = k_ref[0, j], k_ref[1, j]
        z = jnp.concatenate([xr * kr - xi * ki, xr * ki + xi * kr], axis=0)
        u = _dot_table(gi, None, z, False)
        u_ref[0, j] = u[0:n2]
        u_ref[1, j] = u[n2:2 * n2]


def _dft_b(y, kspec, g_fwd, g_inv):
    nb, _, n1, n2, _ = y.shape
    groups = max(1, DFT_ROWS // n2)
    const = lambda i, b: (0, 0)
    return pl.pallas_call(
        functools.partial(_dft_b_kernel, groups=groups),
        grid=(n1 // groups, nb),
        in_specs=[pl.BlockSpec((None, 2, groups, n2, HY_W), lambda i, b: (b, 0, i, 0, 0)),
                  pl.BlockSpec((2, groups, n2, HY_W), lambda i, b: (0, i, 0, 0)),
                  pl.BlockSpec((2 * n2, 2 * n2), const), pl.BlockSpec((2 * n2, 2 * n2), const)],
        out_specs=pl.BlockSpec((None, 2, groups, n2, HY_W), lambda i, b: (b, 0, i, 0, 0)),
        out_shape=jax.ShapeDtypeStruct(y.shape, F32),
        compiler_params=_params(2),
        name="hy_dft_b",
    )(y, kspec, g_fwd, g_inv)


def _dft_a_inv_kernel(u_ref, t_ref, v_ref, x0_ref, bias_ref, gain_ref, o_ref, e_ref):
    @pl.when(pl.program_id(1) == 0)
    def _():
        e_ref[...] = _expand_cols(t_ref[...])

    half, grp, width = v_ref.shape
    rows = half * grp
    u = u_ref[...].reshape(4 * rows, width)
    y = _dot_table(e_ref[...], None, u, False)
    y = (y + v_ref[...].reshape(rows, width) * bias_ref[...]) * x0_ref[...].reshape(rows, width)
    o_ref[...] = (_rms(y) * gain_ref[...]).reshape(o_ref.shape)


def _dft_a_inv(u, table, v, x0, bias, gain):
    nb, half, n2, _ = v.shape
    grp = SUBLANES
    tile = pl.BlockSpec((None, half, grp, HY_W), lambda j, b: (b, 0, j, 0))
    vec = pl.BlockSpec((1, HY_W), lambda j, b: (0, 0))
    return pl.pallas_call(
        _dft_a_inv_kernel,
        grid=(n2 // grp, nb),
        in_specs=[pl.BlockSpec((None, 4 * half, grp, HY_W), lambda j, b: (b, 0, j, 0)),
                  pl.BlockSpec((None, half * grp, 4 * half), lambda j, b: (j, 0, 0)),
                  tile, tile, vec, vec],
        scratch_shapes=[pltpu.VMEM((half * grp, 4 * half * grp), BF16)],
        out_specs=tile,
        out_shape=jax.ShapeDtypeStruct(v.shape, F32),
        compiler_params=_params(2),
        name="hy_dft_a_inv",
    )(u, table, v, x0, bias, gain)


def _hyena(u_hy, p, l, gain, tables):
    b, seq_len, _ = u_hy.shape
    n1 = DFT_N1
    n2 = 2 * seq_len // n1
    v, x0 = _hy_pre(u_hy, p['hy_conv_w'][l], p['hy_conv_b'][l])
    kk = _hy_filter(seq_len, p['hy_w1'][l], p['hy_b1'][l], p['hy_w2'][l], p['hy_b2'][l],
                    p['hy_w3'][l], p['hy_b3'][l], p['hy_w4'][l], p['hy_freq'][l])
    ky = _dft_a(kk.reshape(2, n1 // 2, n2, HY_W), tables['fwd'], True)
    kspec = _dft_b_filter(ky.reshape(2, 2, n1, n2, HY_W), tables['g_fwd'])
    vm = v.reshape(b, n1 // 2, n2, HY_W)
    y = _dft_a(vm, tables['fwd'], False).reshape(b, 2, n1, n2, HY_W)
    u = _dft_b(y, kspec, tables['g_fwd'][0], tables['g_inv']).reshape(b, 2 * n1, n2, HY_W)
    out = _dft_a_inv(u, tables['inv'], vm, x0.reshape(b, n1 // 2, n2, HY_W),
                     p['hy_bias'][l].reshape(1, HY_W), gain.reshape(1, HY_W))
    return out.reshape(b * seq_len, HY_W)


def _cumsum_rows(g):
    n = g.shape[0]
    row = _row_iota((SUBLANES, g.shape[1]))
    s = 1
    while s < n:
        if s < SUBLANES:
            y = pltpu.roll(g, s, 0)
            head = jnp.where(row >= s, y[0:SUBLANES], 0.0)
            g = g + jnp.concatenate([head, y[SUBLANES:]], axis=0)
        else:
            g = jnp.concatenate([g[0:s], g[s:] + g[0:n - s]], axis=0)
        s *= 2
    return g


def _hg_chunk(q, v, z, lb, st, reverse):
    c = q.shape[0]
    e = jnp.exp2(z * (-LOG2E))
    s = 1.0 / (1.0 + e)
    g = jnp.log2(lb + (1.0 - lb) * s)
    kk = (1.0 - lb) * (e * s)
    bq = _cumsum_rows(g)
    total = bq[c - 1:c, :]
    if reverse:
        bq = total - bq + g
    blocks = []
    for i in range(c // HG_SUB):
        lo, hi = i * HG_SUB, (i + 1) * HG_SUB
        ks, ke = (lo, c) if reverse else (0, hi)
        ref = bq[lo + HG_SUB // 2:lo + HG_SUB // 2 + 1, :]
        qt = q[lo:hi] * jnp.exp2(bq[lo:hi] - ref)
        kt = kk[ks:ke] * jnp.exp2(ref - bq[ks:ke])
        parts = ([jnp.zeros((ks, HG_D), F32)] if ks else []) + [kt]
        parts += [jnp.zeros((c - ke, HG_D), F32)] if ke < c else []
        blocks.append(_dot_nt(qt, jnp.concatenate(parts, axis=0)))
    a = jnp.concatenate(blocks, axis=0)
    r2 = _row_iota((c, c))
    c2 = lax.broadcasted_iota(jnp.int32, (c, c), 1)
    a = jnp.where((c2 >= r2) if reverse else (c2 <= r2), a, 0.0)
    o = _dot_nt(q * jnp.exp2(bq), st) + _dot_bf16(a, v)
    st_new = st * jnp.exp2(total) + _dot_tn(v, kk * jnp.exp2(total - bq))
    return o, st_new


def _hg_kernel(*refs, reverse, combine, tb):
    if combine:
        q_ref, v_ref, z_ref, lb_ref, of_ref, g_ref, gain_ref, o_ref, st_ref, ob_ref = refs
    else:
        q_ref, v_ref, z_ref, lb_ref, o_ref, st_ref = refs
        ob_ref = o_ref

    @pl.when(pl.program_id(1) == 0)
    def _():
        st_ref[...] = jnp.zeros(st_ref.shape, F32)

    n_chunks = tb // HG_CHUNK

    def body(ci, carry):
        c = (n_chunks - 1 - ci) if reverse else ci
        r0 = pl.multiple_of(c * HG_CHUNK, HG_CHUNK)
        rows = pl.ds(r0, HG_CHUNK)
        for h in range(HG_HEADS):
            cols = slice(h * HG_D, (h + 1) * HG_D)
            o, st = _hg_chunk(q_ref[rows, cols], v_ref[rows, cols], z_ref[rows, cols],
                              lb_ref[:, cols], st_ref[h], reverse)
            st_ref[h] = st
            ob_ref[rows, cols] = o
        return carry

    lax.fori_loop(0, n_chunks, body, 0, unroll=HG_UNROLL)

    if combine:
        gate = g_ref[...]
        gate = gate * _sigmoid(gate)
        for h in range(HG_HEADS):
            cols = slice(h * HG_D, (h + 1) * HG_D)
            o = of_ref[:, cols] + ob_ref[:, cols]
            o_ref[:, cols] = _rms(o) * gain_ref[:, cols] * gate[:, cols]


def _hg_direction(u_hg, lb, reverse, o_fwd=None, gain=None):
    b, l, _ = u_hg.shape
    tb = min(SEQ_TILE, l)
    nt = l // tb
    combine = o_fwd is not None

    def col(j):
        return pl.BlockSpec((None, tb, HG_W),
                            lambda bb, i: (bb, (nt - 1 - i) if reverse else i, j))

    vec = pl.BlockSpec((1, HG_W), lambda bb, i: (0, 0))
    in_specs = [col(0), col(1), col(3 if reverse else 2), vec]
    args = [u_hg, u_hg, u_hg, lb.reshape(1, HG_W)]
    scratch = [pltpu.VMEM((HG_HEADS, HG_D, HG_D), F32)]
    if combine:
        in_specs += [col(0), col(4), vec]
        args += [o_fwd, u_hg, gain.reshape(1, HG_W)]
        scratch += [pltpu.VMEM((tb, HG_W), F32)]
    return pl.pallas_call(
        functools.partial(_hg_kernel, reverse=reverse, combine=combine, tb=tb),
        grid=(b, nt),
        in_specs=in_specs,
        out_specs=col(0),
        out_shape=jax.ShapeDtypeStruct((b, l, HG_W), F32),
        scratch_shapes=scratch,
        compiler_params=_params(2),
        name="hg_bwd" if reverse else "hg_fwd",
    )(*args)


def _hgrn2(u_hg, lb_f, lb_b, gain):
    b, l, _ = u_hg.shape
    o_f = _hg_direction(u_hg, lb_f, reverse=False)
    y = _hg_direction(u_hg, lb_b, reverse=True, o_fwd=o_f, gain=gain)
    return y.reshape(b * l, HG_W)


def _scan8(a, b, reverse):
    row = _row_iota(a.shape)
    for s in (1, 2, 4):
        if reverse:
            keep = row < SUBLANES - s
            a_s, b_s = pltpu.roll(a, SUBLANES - s, 0), pltpu.roll(b, SUBLANES - s, 0)
        else:
            keep = row >= s
            a_s, b_s = pltpu.roll(a, s, 0), pltpu.roll(b, s, 0)
        b = jnp.where(keep, a * b_s, 0.0) + b
        a = jnp.where(keep, a * a_s, a)
    return a, b


def _rg_kernel(*refs, reverse, combine, tb):
    if combine:
        (u_ref, up_ref, un_ref, cw_ref, cb_ref, wh_ref, wl_ref, gb_ref, lam_ref,
         hf_ref, gain_ref, o_ref, a_ref, b_ref, carry_ref) = refs
        h_ref = b_ref
    else:
        (u_ref, up_ref, un_ref, cw_ref, cb_ref, wh_ref, wl_ref, gb_ref, lam_ref,
         o_ref, a_ref, b_ref, carry_ref) = refs
        h_ref = o_ref
    step = pl.program_id(1)
    n = pl.num_programs(1)
    pos = (n - 1 - step) if reverse else step

    @pl.when(step == 0)
    def _():
        carry_ref[...] = jnp.zeros(carry_ref.shape, F32)

    x = u_ref[:, 0:RG_W]
    has_prev = pos > 0
    has_next = pos < n - 1
    p6 = jnp.where(has_prev, up_ref[SUBLANES - 2:SUBLANES - 1, 0:RG_W], 0.0)
    p7 = jnp.where(has_prev, up_ref[SUBLANES - 1:SUBLANES, 0:RG_W], 0.0)
    nx = jnp.where(has_next, un_ref[0:1, 0:RG_W], 0.0)
    xr = cb_ref[...] + _shift_down(x, [p6, p7]) * cw_ref[0:1, :]
    xr = xr + _shift_down(x, [p7]) * cw_ref[1:2, :]
    xr = xr + x * cw_ref[2:3, :]
    xr = xr + _shift_up(x, nx) * cw_ref[3:4, :]

    x_hi = xr.astype(BF16)
    x_lo = (xr - x_hi.astype(F32)).astype(BF16)
    wh = wh_ref[...]
    pre = (jnp.dot(x_hi, wh, preferred_element_type=F32)
           + jnp.dot(x_lo, wh, preferred_element_type=F32)
           + jnp.dot(x_hi, wl_ref[...], preferred_element_type=F32)) + gb_ref[...]
    r = _sigmoid(pre[:, 0:RG_W])
    ig = _sigmoid(pre[:, RG_W:2 * RG_W])
    nl = -lam_ref[...]
    softplus = jnp.maximum(nl, 0.0) + jnp.log1p(jnp.exp(-jnp.abs(nl)))
    a = jnp.exp(-RG_C * r * softplus)
    a_ref[...] = a
    b_ref[...] = jnp.sqrt(1.0 - a * a) * (ig * xr)

    n_groups = tb // SUBLANES

    def body(gi, carry):
        g = (n_groups - 1 - gi) if reverse else gi
        rows = pl.ds(pl.multiple_of(g * SUBLANES, SUBLANES), SUBLANES)
        a, b = _scan8(a_ref[rows, :], b_ref[rows, :], reverse)
        h = a * carry + b
        h_ref[rows, :] = h
        return h[0:1, :] if reverse else h[SUBLANES - 1:SUBLANES, :]

    carry_ref[...] = lax.fori_loop(0, n_groups, body, carry_ref[...], unroll=RG_UNROLL)

    if combine:
        gate = u_ref[:, RG_W:2 * RG_W]
        gelu = 0.5 * gate * (1.0 + jnp.tanh(math.sqrt(2.0 / math.pi)
                                             * (gate + 0.044715 * (gate * gate * gate))))
        y = (hf_ref[...] + h_ref[...]) * gelu
        o_ref[...] = _rms(y) * gain_ref[...]


def _blockdiag(w):
    h, d, _ = w.shape
    eye = jnp.eye(h, dtype=w.dtype)
    return (eye[:, None, :, None] * w[:, :, None, :]).reshape(h * d, h * d)


def _rg_direction(u_rg, p, l, d, h_fwd=None, gain=None):
    b, seq_len, _ = u_rg.shape
    tb = min(SEQ_TILE, seq_len)
    nt = seq_len // tb
    reverse = d == 1
    combine = h_fwd is not None
    main, prev, nxt = _seq_specs(tb, RG_IN, nt, reverse)
    w = jnp.concatenate([_blockdiag(p['rg_wa'][l, d]), _blockdiag(p['rg_wx'][l, d])], axis=1)
    w_hi = w.astype(BF16)
    w_lo = (w - w_hi.astype(F32)).astype(BF16)
    gb = jnp.concatenate([p['rg_ba'][l, d], p['rg_bx'][l, d]]).reshape(1, 2 * RG_W)
    const = lambda bb, i: (0, 0)
    vec = pl.BlockSpec((1, RG_W), const)
    out = pl.BlockSpec((None, tb, RG_W), lambda bb, i: (bb, (nt - 1 - i) if reverse else i, 0))
    in_specs = [main, prev, nxt, pl.BlockSpec((4, RG_W), const), vec,
                pl.BlockSpec((RG_W, 2 * RG_W), const), pl.BlockSpec((RG_W, 2 * RG_W), const),
                pl.BlockSpec((1, 2 * RG_W), const), vec]
    args = [u_rg, u_rg, u_rg, p['rg_conv_w'][l], p['rg_conv_b'][l].reshape(1, RG_W), w_hi, w_lo, gb,
            p['rg_lambda'][l, d].reshape(1, RG_W)]
    if combine:
        in_specs += [out, vec]
        args += [h_fwd, gain.reshape(1, RG_W)]
    return pl.pallas_call(
        functools.partial(_rg_kernel, reverse=reverse, combine=combine, tb=tb),
        grid=(b, nt),
        in_specs=in_specs,
        out_specs=out,
        out_shape=jax.ShapeDtypeStruct((b, seq_len, RG_W), F32),
        scratch_shapes=[pltpu.VMEM((tb, RG_W), F32), pltpu.VMEM((tb, RG_W), F32),
                        pltpu.VMEM((1, RG_W), F32)],
        compiler_params=_params(2),
        name="rg_bwd" if reverse else "rg_fwd",
    )(*args)


def _rglru(u_rg, p, l, gain):
    b, seq_len, _ = u_rg.shape
    h_f = _rg_direction(u_rg, p, l, 0)
    y = _rg_direction(u_rg, p, l, 1, h_fwd=h_f, gain=gain)
    return y.reshape(b * seq_len, RG_W)


def _outffn_kernel(*refs, final, ff_chunks):
    if final:
        (x_ref, yhy_ref, yhg_ref, yrg_ref, g1_ref, sh2_ref, sc2_ref, g2_ref, n2_ref,
         wo_ref, wg_ref, wu_ref, wd_ref, fn_ref, o_ref) = refs
    else:
        (x_ref, yhy_ref, yhg_ref, yrg_ref, g1_ref, sh2_ref, sc2_ref, g2_ref, n2_ref,
         wo_ref, wg_ref, wu_ref, wd_ref, o_ref) = refs
    mix = jnp.dot(yhy_ref[...].astype(BF16), wo_ref[0:HY_W, :], preferred_element_type=F32)
    mix = mix + jnp.dot(yhg_ref[...].astype(BF16), wo_ref[HY_W:HY_W + HG_W, :],
                        preferred_element_type=F32)
    mix = mix + jnp.dot(yrg_ref[...].astype(BF16), wo_ref[HY_W + HG_W:D_MODEL, :],
                        preferred_element_type=F32)
    x1 = x_ref[...] + g1_ref[...] * mix
    h = _rms(x1) * n2_ref[...]
    h = (h * (1.0 + sc2_ref[...]) + sh2_ref[...]).astype(BF16)
    step = D_FF // ff_chunks
    ff = None
    for c in range(ff_chunks):
        sl = slice(c * step, (c + 1) * step)
        gate = jnp.dot(h, wg_ref[:, sl], preferred_element_type=F32)
        up = jnp.dot(h, wu_ref[:, sl], preferred_element_type=F32)
        act = ((gate * _sigmoid(gate)) * up).astype(BF16)
        part = jnp.dot(act, wd_ref[sl, :], preferred_element_type=F32)
        ff = part if ff is None else ff + part
    x2 = x1 + g2_ref[...] * ff
    if final:
        x2 = _rms(x2) * fn_ref[...]
    o_ref[...] = x2


def _outffn(x2, y_hy, y_hg, y_rg, mod4, norm_w, w_out, wg, wu, wd, seq_len, final_w=None):
    t = x2.shape[0]
    tm = min(TOKEN_TILE, seq_len)
    tpb = seq_len // tm
    final = final_w is not None
    row = lambda i: (i, 0)
    const = lambda i: (0, 0)

    def resident(shape):
        return pl.BlockSpec(shape, const, pipeline_mode=pl.Buffered(1))

    in_specs = [pl.BlockSpec((tm, D_MODEL), row), pl.BlockSpec((tm, HY_W), row),
                pl.BlockSpec((tm, HG_W), row), pl.BlockSpec((tm, RG_W), row),
                _mod_spec(2, tpb), _mod_spec(3, tpb), _mod_spec(4, tpb), _mod_spec(5, tpb),
                pl.BlockSpec((1, D_MODEL), const),
                resident((D_MODEL, D_MODEL)), resident((D_MODEL, D_FF)), resident((D_MODEL, D_FF)),
                resident((D_FF, D_MODEL))]
    args = [x2, y_hy, y_hg, y_rg, mod4, mod4, mod4, mod4, norm_w.reshape(1, D_MODEL),
            w_out, wg, wu, wd]
    if final:
        in_specs.append(pl.BlockSpec((1, D_MODEL), const))
        args.append(final_w.reshape(1, D_MODEL))
    return pl.pallas_call(
        functools.partial(_outffn_kernel, final=final, ff_chunks=2),
        grid=(t // tm,),
        in_specs=in_specs,
        out_specs=pl.BlockSpec((tm, D_MODEL), row),
        out_shape=jax.ShapeDtypeStruct((t, D_MODEL), F32),
        compiler_params=_params(1),
        name="outffn",
    )(*args)


def _trunk(x, mods, lb_all, p, wb):
    b, seq_len, _ = x.shape
    x2 = x.reshape(b * seq_len, D_MODEL)
    tables = _dft_tables(seq_len)
    for l in range(DEPTH):
        mod4 = mods[l].reshape(b, 6, 1, D_MODEL)
        gain = p['out_norm_w'][l]
        u_hy, u_hg, u_rg = _inproj(x2, mod4, p['norm1_w'][l], wb['w_in'][l], seq_len)
        y_hy = _hyena(u_hy.reshape(b, seq_len, HY_IN), p, l, gain[0:HY_W], tables)
        y_hg = _hgrn2(u_hg.reshape(b, seq_len, HG_IN), lb_all[0, l], lb_all[1, l],
                      gain[HY_W:HY_W + HG_W])
        y_rg = _rglru(u_rg.reshape(b, seq_len, RG_IN), p, l, gain[HY_W + HG_W:])
        x2 = _outffn(x2, y_hy, y_hg, y_rg, mod4, p['norm2_w'][l], wb['w_out'][l], wb['ffn_wg'][l],
                     wb['ffn_wu'][l], wb['ffn_wd'][l], seq_len,
                     final_w=p['final_norm_w'] if l == DEPTH - 1 else None)
    return x2.reshape(b, seq_len, D_MODEL)


def kernel(x_prompt, x_sample, c_prompt, c_sample, w_in, w_out, out_norm_w, ada_w, ada_b, norm1_w, norm2_w, final_norm_w, hy_conv_w, hy_conv_b, hy_w1, hy_b1, hy_w2, hy_b2, hy_w3, hy_b3, hy_w4, hy_freq, hy_bias, hg_lb_logits, rg_conv_w, rg_conv_b, rg_wa, rg_ba, rg_wx, rg_bx, rg_lambda, ffn_wg, ffn_wu, ffn_wd):
    p = dict(out_norm_w=out_norm_w, norm1_w=norm1_w, norm2_w=norm2_w, final_norm_w=final_norm_w,
             hy_conv_w=hy_conv_w, hy_conv_b=hy_conv_b, hy_w1=hy_w1, hy_b1=hy_b1, hy_w2=hy_w2,
             hy_b2=hy_b2, hy_w3=hy_w3, hy_b3=hy_b3, hy_w4=hy_w4, hy_freq=hy_freq, hy_bias=hy_bias,
             rg_conv_w=rg_conv_w, rg_conv_b=rg_conv_b, rg_wa=rg_wa, rg_ba=rg_ba, rg_wx=rg_wx,
             rg_bx=rg_bx, rg_lambda=rg_lambda)
    wb = dict(w_in=w_in.astype(BF16), w_out=w_out.astype(BF16), ffn_wg=ffn_wg.astype(BF16),
              ffn_wu=ffn_wu.astype(BF16), ffn_wd=ffn_wd.astype(BF16))
    lb_all = jax.nn.softmax(hg_lb_logits.astype(F32), axis=1)
    lb_all = jnp.cumsum(lb_all, axis=1) - lb_all[:, :1]
    nb_p, nb_s = c_prompt.shape[0], c_sample.shape[0]
    rows = -(-(nb_p + nb_s) // SUBLANES) * SUBLANES
    c_all = jnp.pad(jnp.concatenate([c_prompt, c_sample]), ((0, rows - nb_p - nb_s), (0, 0)))
    mods = _ada(c_all, ada_w, ada_b)
    y_prompt = _trunk(x_prompt, mods[:, 0:nb_p], lb_all, p, wb)
    y_sample = _trunk(x_sample, mods[:, nb_p:nb_p + nb_s], lb_all, p, wb)
    return (y_prompt, y_sample)
```

```python
import functools
import math

import jax
import jax.numpy as jnp
from jax import lax
from jax.experimental import pallas as pl
from jax.experimental.pallas import tpu as pltpu

F32 = jnp.float32
BF16 = jnp.bfloat16
HIGHEST = lax.Precision.HIGHEST

D_MODEL = 1024
DEPTH = 2
HY_W = 256
HG_W = 512
RG_W = 256
HY_IN = 3 * HY_W
HG_IN = 5 * HG_W
RG_IN = 2 * RG_W
D_IN = HY_IN + HG_IN + RG_IN
HY_EMB = 33
HY_BANDS = 16
HY_FFN = 64
HY_FAST_DECAY = 0.3
HY_SLOW_DECAY = 1.5
HY_TARGET = 1e-2
HG_HEADS = 4
HG_D = 128
RG_HEADS = 4
RG_HD = 64
RG_C = 8.0
D_FF = 2816
EPS = 1e-6
LOG2E = 1.4426950408889634

LANES = 128
SUBLANES = 8
VMEM_LIMIT = 56 * 1024 * 1024

TOKEN_TILE = 512
SEQ_TILE = 512
HG_CHUNK = 64
HG_SUB = 32
HG_UNROLL = 8
RG_UNROLL = 8
DFT_N1 = 128
DFT_ROWS = 2048
DFT_GROUPS = 2


def _params(n_axes):
    return pltpu.CompilerParams(dimension_semantics=("arbitrary",) * n_axes,
                                vmem_limit_bytes=VMEM_LIMIT)


def _dot_bf16(a, b):
    return jnp.dot(a.astype(BF16), b.astype(BF16), preferred_element_type=F32)


def _dot_f32(a, b):
    return jnp.dot(a, b, preferred_element_type=F32, precision=HIGHEST)


def _dot_nt(a, b):
    return lax.dot_general(a.astype(BF16), b.astype(BF16), (((1,), (1,)), ((), ())),
                           preferred_element_type=F32)


def _dot_tn(a, b):
    return lax.dot_general(a.astype(BF16), b.astype(BF16), (((0,), (0,)), ((), ())),
                           preferred_element_type=F32)


def _rms(x):
    return x * lax.rsqrt(jnp.mean(x * x, axis=-1, keepdims=True) + EPS)


def _sigmoid(x):
    return 1.0 / (1.0 + jnp.exp(-x))


def _row_iota(shape):
    return lax.broadcasted_iota(jnp.int32, shape, 0)


def _shift_down(x, fill_rows):
    s = len(fill_rows)
    y = pltpu.roll(x, s, 0)
    head = y[0:SUBLANES]
    row = _row_iota(head.shape)
    for j, r in enumerate(fill_rows):
        head = jnp.where(row == j, r, head)
    return jnp.concatenate([head, y[SUBLANES:]], axis=0)


def _shift_up(x, next_row):
    n = x.shape[0]
    y = pltpu.roll(x, n - 1, 0)
    tail = y[n - SUBLANES:n]
    tail = jnp.where(_row_iota(tail.shape) == SUBLANES - 1, next_row, tail)
    return jnp.concatenate([y[0:n - SUBLANES], tail], axis=0)


def _ada_kernel(c_ref, w_ref, b_ref, o_ref):
    c = c_ref[...]
    s = c * _sigmoid(c)
    o_ref[0] = _dot_f32(s, w_ref[0]) + b_ref[0]


def _ada(c_all, ada_w, ada_b):
    rows = c_all.shape[0]
    n_out = ada_w.shape[-1]
    tn = n_out // 4
    return pl.pallas_call(
        _ada_kernel,
        grid=(DEPTH, n_out // tn),
        in_specs=[pl.BlockSpec((rows, D_MODEL), lambda l, j: (0, 0)),
                  pl.BlockSpec((1, D_MODEL, tn), lambda l, j: (l, 0, j)),
                  pl.BlockSpec((1, 1, tn), lambda l, j: (l, 0, j))],
        out_specs=pl.BlockSpec((1, rows, tn), lambda l, j: (l, 0, j)),
        out_shape=jax.ShapeDtypeStruct((DEPTH, rows, n_out), F32),
        compiler_params=_params(2),
        name="ada",
    )(c_all, ada_w, ada_b.reshape(DEPTH, 1, n_out))


def _mod_spec(j, tiles_per_batch):
    return pl.BlockSpec((None, None, 1, D_MODEL), lambda i: (i // tiles_per_batch, j, 0, 0))


def _inproj_kernel(x_ref, nw_ref, sh_ref, sc_ref, w_ref, ohy_ref, ohg_ref, org_ref):
    h = _rms(x_ref[...]) * nw_ref[...]
    h = (h * (1.0 + sc_ref[...]) + sh_ref[...]).astype(BF16)
    ohy_ref[...] = jnp.dot(h, w_ref[:, 0:HY_IN], preferred_element_type=F32)
    ohg_ref[...] = jnp.dot(h, w_ref[:, HY_IN:HY_IN + HG_IN], preferred_element_type=F32)
    org_ref[...] = jnp.dot(h, w_ref[:, HY_IN + HG_IN:D_IN], preferred_element_type=F32)


def _inproj(x2, mod4, norm_w, w_in_bf16, seq_len):
    t = x2.shape[0]
    tm = min(TOKEN_TILE, seq_len)
    tpb = seq_len // tm
    row = lambda i: (i, 0)
    const = lambda i: (0, 0)
    return pl.pallas_call(
        _inproj_kernel,
        grid=(t // tm,),
        in_specs=[pl.BlockSpec((tm, D_MODEL), row),
                  pl.BlockSpec((1, D_MODEL), const),
                  _mod_spec(0, tpb), _mod_spec(1, tpb),
                  pl.BlockSpec((D_MODEL, D_IN), const)],
        out_specs=[pl.BlockSpec((tm, HY_IN), row), pl.BlockSpec((tm, HG_IN), row),
                   pl.BlockSpec((tm, RG_IN), row)],
        out_shape=[jax.ShapeDtypeStruct((t, HY_IN), F32), jax.ShapeDtypeStruct((t, HG_IN), F32),
                   jax.ShapeDtypeStruct((t, RG_IN), F32)],
        compiler_params=_params(1),
        name="inproj",
    )(x2, norm_w.reshape(1, D_MODEL), mod4, mod4, w_in_bf16)


def _seq_specs(tb, width, n_tiles, reverse=False):
    r8 = tb // SUBLANES
    last8 = n_tiles * r8 - 1

    def pos(i):
        return (n_tiles - 1 - i) if reverse else i

    main = pl.BlockSpec((None, tb, width), lambda b, i: (b, pos(i), 0))
    prev = pl.BlockSpec((None, SUBLANES, width),
                        lambda b, i: (b, jnp.maximum(pos(i) * r8 - 1, 0), 0))
    nxt = pl.BlockSpec((None, SUBLANES, width),
                       lambda b, i: (b, jnp.minimum((pos(i) + 1) * r8, last8), 0))
    return main, prev, nxt


def _hy_pre_kernel(u_ref, up_ref, un_ref, cw_ref, cb_ref, v_ref, x0_ref):
    i = pl.program_id(1)
    n = pl.num_programs(1)
    x = u_ref[...]
    prev = jnp.where(i > 0, up_ref[SUBLANES - 1:SUBLANES, :], 0.0)
    nxt = jnp.where(i < n - 1, un_ref[0:1, :], 0.0)
    uc = cb_ref[...] + _shift_down(x, [prev]) * cw_ref[0:1, :]
    uc = uc + x * cw_ref[1:2, :]
    uc = uc + _shift_up(x, nxt) * cw_ref[2:3, :]
    x0_ref[...] = uc[:, 0:HY_W]
    v_ref[...] = uc[:, 2 * HY_W:3 * HY_W] * uc[:, HY_W:2 * HY_W]


def _hy_pre(u_hy, conv_w, conv_b):
    b, l, _ = u_hy.shape
    tb = min(SEQ_TILE, l)
    nt = l // tb
    main, prev, nxt = _seq_specs(tb, HY_IN, nt)
    out = pl.BlockSpec((None, tb, HY_W), lambda bb, i: (bb, i, 0))
    return pl.pallas_call(
        _hy_pre_kernel,
        grid=(b, nt),
        in_specs=[main, prev, nxt,
                  pl.BlockSpec((3, HY_IN), lambda bb, i: (0, 0)),
                  pl.BlockSpec((1, HY_IN), lambda bb, i: (0, 0))],
        out_specs=[out, out],
        out_shape=[jax.ShapeDtypeStruct((b, l, HY_W), F32)] * 2,
        compiler_params=_params(2),
        name="hy_pre",
    )(u_hy, u_hy, u_hy, conv_w, conv_b.reshape(1, HY_IN))


def _hy_filter_kernel(fb_ref, w1t_ref, w1f_ref, b1_ref, w2_ref, b2_ref, w3_ref, b3_ref, w4_ref,
                      fr_ref, dl_ref, k_ref, *, seq_len, tile):
    half = tile // 2
    base = pl.program_id(0) * tile
    lane = lax.broadcasted_iota(jnp.int32, (half, LANES), 1)
    n = (_row_iota((half, LANES)) + base + jnp.where(lane >= HY_FFN, half, 0)).astype(F32)
    t = n / float(seq_len - 1)
    ang = fb_ref[...] * n * (2.0 * math.pi / seq_len)
    feat = jnp.where((lane & (HY_FFN - 1)) < HY_BANDS, jnp.cos(ang), -jnp.sin(ang))
    fr = fr_ref[...]
    h = jnp.sin(fr * (t * w1t_ref[...] + _dot_f32(feat, w1f_ref[...]) + b1_ref[...]))
    h = jnp.sin(fr * (_dot_f32(h, w2_ref[...]) + b2_ref[...]))
    h = jnp.sin(fr * (_dot_f32(h, w3_ref[...]) + b3_ref[...]))
    k = jnp.concatenate([_dot_f32(h, w4_ref[0]), _dot_f32(h, w4_ref[1])], axis=0)
    n2 = (_row_iota((tile, HY_W)) + base)
    decay = jnp.exp(-(n2.astype(F32) / float(seq_len - 1)) * dl_ref[...])
    k_ref[0] = k[:, 0:HY_W] * decay
    k_ref[1] = jnp.where(n2 == 0, 0.0, k[:, HY_W:2 * HY_W] * decay)


def _pad2(a, rows, cols):
    return jnp.pad(a, ((0, rows - a.shape[0]), (0, cols - a.shape[1])))


def _hy_filter(seq_len, w1, b1, w2, b2, w3, b3, w4, freq):
    tile = min(SEQ_TILE, seq_len)
    fb = jnp.linspace(1e-4, HY_BANDS - 1, HY_BANDS, dtype=F32)
    deltas = jnp.abs(jnp.linspace(math.log(HY_TARGET) / HY_SLOW_DECAY,
                                  math.log(HY_TARGET) / HY_FAST_DECAY, HY_W, dtype=F32))[None, :]
    zeros = jnp.zeros((HY_FFN, HY_FFN), F32)
    twice = lambda a: jnp.concatenate([a, a])[None, :]
    diag2 = lambda a: jnp.concatenate([jnp.concatenate([a, zeros], 1),
                                       jnp.concatenate([zeros, a], 1)], 0)
    w4z = jnp.zeros_like(w4)
    args = (twice(_pad2(jnp.concatenate([fb, fb])[None, :], 1, HY_FFN)[0]), twice(w1[0]),
            diag2(_pad2(w1[1:], HY_FFN, HY_FFN)), twice(b1), diag2(w2), twice(b2), diag2(w3), twice(b3),
            jnp.stack([jnp.concatenate([w4, w4z], 0), jnp.concatenate([w4z, w4], 0)]), twice(freq),
            deltas)
    const = lambda i: (0,) * 2
    return pl.pallas_call(
        functools.partial(_hy_filter_kernel, seq_len=seq_len, tile=tile),
        grid=(seq_len // tile,),
        in_specs=[pl.BlockSpec(a.shape, (lambda i: (0, 0, 0)) if a.ndim == 3 else const)
                  for a in args],
        out_specs=pl.BlockSpec((2, tile, HY_W), lambda i: (0, i, 0)),
        out_shape=jax.ShapeDtypeStruct((2, seq_len, HY_W), F32),
        compiler_params=_params(1),
        name="hy_filter",
    )(*args)


def _dft_tables(seq_len):
    n = 2 * seq_len
    n1 = DFT_N1
    n2 = n // n1
    grp = jnp.arange(n2 // SUBLANES, dtype=jnp.int32)[:, None, None, None]
    k1 = jnp.arange(n1, dtype=jnp.int32)[None, :, None, None]
    g = jnp.arange(SUBLANES, dtype=jnp.int32)[None, None, :, None]
    t1 = jnp.arange(n1 // 2, dtype=jnp.int32)[None, None, None, :]
    th = ((k1 * (n2 * t1 + SUBLANES * grp + g)) % n).astype(F32) * (2.0 * math.pi / n)
    trig = jnp.concatenate([jnp.cos(th), -jnp.sin(th)], axis=1)
    groups = n2 // SUBLANES
    fwd = trig.astype(BF16).reshape(groups, 2 * n1 * SUBLANES, n1 // 2)
    inv = jnp.transpose((trig * (1.0 / n)).astype(BF16), (0, 3, 2, 1))
    inv = inv.reshape(groups, (n1 // 2) * SUBLANES, 2 * n1)
    a = jnp.arange(n2, dtype=jnp.int32)
    ph = ((a[:, None] * a[None, :]) % n2).astype(F32) * (2.0 * math.pi / n2)
    c, s = jnp.cos(ph), jnp.sin(ph)
    g_fwd = jnp.concatenate([jnp.concatenate([c, s], 1), jnp.concatenate([-s, c], 1)], 0)
    g_inv = jnp.concatenate([jnp.concatenate([c, -s], 1), jnp.concatenate([s, c], 1)], 0)
    return dict(fwd=fwd, inv=inv, g_fwd=g_fwd.astype(BF16), g_inv=g_inv.astype(BF16))


def _expand_cols(t):
    rows, k = t.shape
    wide = k * SUBLANES
    rep = (lax.broadcasted_iota(jnp.int32, (k, wide), 1) // SUBLANES
           == lax.broadcasted_iota(jnp.int32, (k, wide), 0))
    e = jnp.dot(t, jnp.where(rep, 1.0, 0.0).astype(BF16), preferred_element_type=F32)
    keep = ((_row_iota((rows, wide)) & (SUBLANES - 1))
            == (lax.broadcasted_iota(jnp.int32, (rows, wide), 1) & (SUBLANES - 1)))
    return jnp.where(keep, e, 0.0).astype(BF16)


def _dot_table(table, x):
    return jnp.dot(table, x.astype(BF16), preferred_element_type=F32)


def _dft_a_kernel(x_ref, t_ref, y_ref, e_ref):
    @pl.when(pl.program_id(1) == 0)
    def _():
        for s in range(DFT_GROUPS):
            e_ref[s] = _expand_cols(t_ref[s])

    half, _, width = x_ref.shape
    parts = []
    for s in range(DFT_GROUPS):
        xs = x_ref[:, s * SUBLANES:(s + 1) * SUBLANES, :].reshape(half * SUBLANES, width)
        parts.append(_dot_table(e_ref[s], xs).reshape(4 * half, SUBLANES, width))
    y_ref[...] = jnp.concatenate(parts, axis=1).astype(BF16)


def _dft_a(x, table):
    nb, half, n2, _ = x.shape
    cols = DFT_GROUPS * SUBLANES
    return pl.pallas_call(
        _dft_a_kernel,
        grid=(n2 // cols, nb),
        in_specs=[pl.BlockSpec((None, half, cols, HY_W), lambda j, b: (b, 0, j, 0)),
                  pl.BlockSpec((DFT_GROUPS, 4 * half * SUBLANES, half), lambda j, b: (j, 0, 0))],
        out_specs=pl.BlockSpec((None, 4 * half, cols, HY_W), lambda j, b: (b, 0, j, 0)),
        out_shape=jax.ShapeDtypeStruct((nb, 4 * half, n2, HY_W), BF16),
        scratch_shapes=[pltpu.VMEM((DFT_GROUPS, 4 * half * SUBLANES, half * SUBLANES), BF16)],
        compiler_params=_params(2),
        name="hy_dft_a",
    )(x, table)


def _dft_b_filter_kernel(y_ref, g_ref, k_ref, *, groups):
    g = g_ref[...]
    n2 = y_ref.shape[-2]
    for j in range(groups):
        xf = _dot_table(g, jnp.concatenate([y_ref[0, 0, j], y_ref[0, 1, j]], axis=0))
        xb = _dot_table(g, jnp.concatenate([y_ref[1, 0, j], y_ref[1, 1, j]], axis=0))
        k_ref[0, j] = xf[0:n2] + xb[0:n2]
        k_ref[1, j] = xf[n2:2 * n2] - xb[n2:2 * n2]


def _dft_b_filter(y, g_fwd):
    _, _, n1, n2, _ = y.shape
    groups = max(1, DFT_ROWS // n2)
    const = lambda i: (0, 0)
    return pl.pallas_call(
        functools.partial(_dft_b_filter_kernel, groups=groups),
        grid=(n1 // groups,),
        in_specs=[pl.BlockSpec((2, 2, groups, n2, HY_W), lambda i: (0, 0, i, 0, 0)),
                  pl.BlockSpec((2 * n2, 2 * n2), const)],
        out_specs=pl.BlockSpec((2, groups, n2, HY_W), lambda i: (0, i, 0, 0)),
        out_shape=jax.ShapeDtypeStruct((2, n1, n2, HY_W), F32),
        compiler_params=_params(1),
        name="hy_dft_b_filter",
    )(y, g_fwd)


def _dft_b_kernel(y_ref, k_ref, gf_ref, gi_ref, u_ref, *, groups):
    gf = gf_ref[...]
    gi = gi_ref[...]
    n2 = y_ref.shape[-2]
    for j in range(groups):
        x = _dot_table(gf, jnp.concatenate([y_ref[0, j], y_ref[1, j]], axis=0))
        xr, xi = x[0:n2], x[n2:2 * n2]
        kr, ki = k_ref[0, j], k_ref[1, j]
        z = jnp.concatenate([xr * kr - xi * ki, xr * ki + xi * kr], axis=0)
        u = _dot_table(gi, z).astype(BF16)
        u_ref[0, j] = u[0:n2]
        u_ref[1, j] = u[n2:2 * n2]


def _dft_b(y, kspec, g_fwd, g_inv):
    nb, _, n1, n2, _ = y.shape
    groups = max(1, DFT_ROWS // n2)
    const = lambda i, b: (0, 0)
    return pl.pallas_call(
        functools.partial(_dft_b_kernel, groups=groups),
        grid=(n1 // groups, nb),
        in_specs=[pl.BlockSpec((None, 2, groups, n2, HY_W), lambda i, b: (b, 0, i, 0, 0)),
                  pl.BlockSpec((2, groups, n2, HY_W), lambda i, b: (0, i, 0, 0)),
                  pl.BlockSpec((2 * n2, 2 * n2), const), pl.BlockSpec((2 * n2, 2 * n2), const)],
        out_specs=pl.BlockSpec((None, 2, groups, n2, HY_W), lambda i, b: (b, 0, i, 0, 0)),
        out_shape=jax.ShapeDtypeStruct(y.shape, BF16),
        compiler_params=_params(2),
        name="hy_dft_b",
    )(y, kspec, g_fwd, g_inv)


def _dft_a_inv_kernel(u_ref, t_ref, v_ref, x0_ref, bias_ref, gain_ref, o_ref, e_ref):
    @pl.when(pl.program_id(1) == 0)
    def _():
        for s in range(DFT_GROUPS):
            e_ref[s] = _expand_cols(t_ref[s])

    half, _, width = v_ref.shape
    rows = half * SUBLANES
    u = u_ref[...].astype(F32)
    for s in range(DFT_GROUPS):
        sl = slice(s * SUBLANES, (s + 1) * SUBLANES)
        y = _dot_table(e_ref[s], u[:, sl, :].reshape(4 * rows, width))
        y = ((y + v_ref[:, sl, :].reshape(rows, width) * bias_ref[...])
             * x0_ref[:, sl, :].reshape(rows, width))
        o_ref[:, sl, :] = (_rms(y) * gain_ref[...]).reshape(half, SUBLANES, width)


def _dft_a_inv(u, table, v, x0, bias, gain):
    nb, half, n2, _ = v.shape
    grp = DFT_GROUPS * SUBLANES
    tile = pl.BlockSpec((None, half, grp, HY_W), lambda j, b: (b, 0, j, 0))
    vec = pl.BlockSpec((1, HY_W), lambda j, b: (0, 0))
    return pl.pallas_call(
        _dft_a_inv_kernel,
        grid=(n2 // grp, nb),
        in_specs=[pl.BlockSpec((None, 4 * half, grp, HY_W), lambda j, b: (b, 0, j, 0)),
                  pl.BlockSpec((DFT_GROUPS, half * SUBLANES, 4 * half), lambda j, b: (j, 0, 0)),
                  tile, tile, vec, vec],
        scratch_shapes=[pltpu.VMEM((DFT_GROUPS, half * SUBLANES, 4 * half * SUBLANES), BF16)],
        out_specs=tile,
        out_shape=jax.ShapeDtypeStruct(v.shape, F32),
        compiler_params=_params(2),
        name="hy_dft_a_inv",
    )(u, table, v, x0, bias, gain)


def _hyena(u_hy, p, l, gain, tables):
    b, seq_len, _ = u_hy.shape
    n1 = DFT_N1
    n2 = 2 * seq_len // n1
    v, x0 = _hy_pre(u_hy, p['hy_conv_w'][l], p['hy_conv_b'][l])
    kk = _hy_filter(seq_len, p['hy_w1'][l], p['hy_b1'][l], p['hy_w2'][l], p['hy_b2'][l],
                    p['hy_w3'][l], p['hy_b3'][l], p['hy_w4'][l], p['hy_freq'][l])
    ky = _dft_a(kk.reshape(2, n1 // 2, n2, HY_W), tables['fwd'])
    kspec = _dft_b_filter(ky.reshape(2, 2, n1, n2, HY_W), tables['g_fwd'])
    vm = v.reshape(b, n1 // 2, n2, HY_W)
    y = _dft_a(vm, tables['fwd']).reshape(b, 2, n1, n2, HY_W)
    u = _dft_b(y, kspec, tables['g_fwd'], tables['g_inv']).reshape(b, 2 * n1, n2, HY_W)
    out = _dft_a_inv(u, tables['inv'], vm, x0.reshape(b, n1 // 2, n2, HY_W),
                     p['hy_bias'][l].reshape(1, HY_W), gain.reshape(1, HY_W))
    return out.reshape(b * seq_len, HY_W)


def _cumsum_rows(g):
    n = g.shape[0]
    row = _row_iota((SUBLANES, g.shape[1]))
    s = 1
    while s < n:
        if s < SUBLANES:
            y = pltpu.roll(g, s, 0)
            head = jnp.where(row >= s, y[0:SUBLANES], 0.0)
            g = g + jnp.concatenate([head, y[SUBLANES:]], axis=0)
        else:
            g = jnp.concatenate([g[0:s], g[s:] + g[0:n - s]], axis=0)
        s *= 2
    return g


def _hg_chunk(q, v, z, lb, st, reverse):
    c = q.shape[0]
    e = jnp.exp2(z * (-LOG2E))
    s = 1.0 / (1.0 + e)
    g = jnp.log2(lb + (1.0 - lb) * s)
    kk = (1.0 - lb) * (e * s)
    bq = _cumsum_rows(g)
    total = bq[c - 1:c, :]
    if reverse:
        bq = total - bq + g
    blocks = []
    for i in range(c // HG_SUB):
        lo, hi = i * HG_SUB, (i + 1) * HG_SUB
        ks, ke = (lo, c) if reverse else (0, hi)
        ref = bq[lo + HG_SUB // 2:lo + HG_SUB // 2 + 1, :]
        qt = q[lo:hi] * jnp.exp2(bq[lo:hi] - ref)
        kt = kk[ks:ke] * jnp.exp2(ref - bq[ks:ke])
        parts = ([jnp.zeros((ks, HG_D), F32)] if ks else []) + [kt]
        parts += [jnp.zeros((c - ke, HG_D), F32)] if ke < c else []
        blocks.append(_dot_nt(qt, jnp.concatenate(parts, axis=0)))
    a = jnp.concatenate(blocks, axis=0)
    r2 = _row_iota((c, c))
    c2 = lax.broadcasted_iota(jnp.int32, (c, c), 1)
    a = jnp.where((c2 >= r2) if reverse else (c2 <= r2), a, 0.0)
    o = _dot_nt(q * jnp.exp2(bq), st) + _dot_bf16(a, v)
    st_new = st * jnp.exp2(total) + _dot_tn(v, kk * jnp.exp2(total - bq))
    return o, st_new


def _hg_kernel(*refs, reverse, combine, tb):
    if combine:
        q_ref, v_ref, z_ref, lb_ref, of_ref, g_ref, gain_ref, o_ref, st_ref, ob_ref = refs
    else:
        q_ref, v_ref, z_ref, lb_ref, o_ref, st_ref = refs
        ob_ref = o_ref

    @pl.when(pl.program_id(1) == 0)
    def _():
        st_ref[...] = jnp.zeros(st_ref.shape, F32)

    n_chunks = tb // HG_CHUNK

    def body(ci, carry):
        c = (n_chunks - 1 - ci) if reverse else ci
        r0 = pl.multiple_of(c * HG_CHUNK, HG_CHUNK)
        rows = pl.ds(r0, HG_CHUNK)
        for h in range(HG_HEADS):
            cols = slice(h * HG_D, (h + 1) * HG_D)
            o, st = _hg_chunk(q_ref[rows, cols], v_ref[rows, cols], z_ref[rows, cols],
                              lb_ref[:, cols], st_ref[h], reverse)
            st_ref[h] = st
            ob_ref[rows, cols] = o
        return carry

    lax.fori_loop(0, n_chunks, body, 0, unroll=HG_UNROLL)

    if combine:
        gate = g_ref[...]
        gate = gate * _sigmoid(gate)
        for h in range(HG_HEADS):
            cols = slice(h * HG_D, (h + 1) * HG_D)
            o = of_ref[:, cols] + ob_ref[:, cols]
            o_ref[:, cols] = _rms(o) * gain_ref[:, cols] * gate[:, cols]


def _hg_direction(u_hg, lb, reverse, o_fwd=None, gain=None):
    b, l, _ = u_hg.shape
    tb = min(SEQ_TILE, l)
    nt = l // tb
    combine = o_fwd is not None

    def col(j):
        return pl.BlockSpec((None, tb, HG_W),
                            lambda bb, i: (bb, (nt - 1 - i) if reverse else i, j))

    vec = pl.BlockSpec((1, HG_W), lambda bb, i: (0, 0))
    in_specs = [col(0), col(1), col(3 if reverse else 2), vec]
    args = [u_hg, u_hg, u_hg, lb.reshape(1, HG_W)]
    scratch = [pltpu.VMEM((HG_HEADS, HG_D, HG_D), F32)]
    if combine:
        in_specs += [col(0), col(4), vec]
        args += [o_fwd, u_hg, gain.reshape(1, HG_W)]
        scratch += [pltpu.VMEM((tb, HG_W), F32)]
    return pl.pallas_call(
        functools.partial(_hg_kernel, reverse=reverse, combine=combine, tb=tb),
        grid=(b, nt),
        in_specs=in_specs,
        out_specs=col(0),
        out_shape=jax.ShapeDtypeStruct((b, l, HG_W), F32),
        scratch_shapes=scratch,
        compiler_params=_params(2),
        name="hg_bwd" if reverse else "hg_fwd",
    )(*args)


def _hgrn2(u_hg, lb_f, lb_b, gain):
    b, l, _ = u_hg.shape
    o_f = _hg_direction(u_hg, lb_f, reverse=False)
    y = _hg_direction(u_hg, lb_b, reverse=True, o_fwd=o_f, gain=gain)
    return y.reshape(b * l, HG_W)


def _scan8(a, b, reverse):
    row = _row_iota(a.shape)
    for s in (1, 2, 4):
        if reverse:
            keep = row < SUBLANES - s
            a_s, b_s = pltpu.roll(a, SUBLANES - s, 0), pltpu.roll(b, SUBLANES - s, 0)
        else:
            keep = row >= s
            a_s, b_s = pltpu.roll(a, s, 0), pltpu.roll(b, s, 0)
        b = jnp.where(keep, a * b_s, 0.0) + b
        a = jnp.where(keep, a * a_s, a)
    return a, b


def _rg_kernel(*refs, reverse, combine, tb):
    if combine:
        (u_ref, up_ref, un_ref, cw_ref, cb_ref, wh_ref, wl_ref, gb_ref, lam_ref,
         hf_ref, gain_ref, o_ref, a_ref, b_ref, carry_ref) = refs
        h_ref = b_ref
    else:
        (u_ref, up_ref, un_ref, cw_ref, cb_ref, wh_ref, wl_ref, gb_ref, lam_ref,
         o_ref, a_ref, b_ref, carry_ref) = refs
        h_ref = o_ref
    step = pl.program_id(1)
    n = pl.num_programs(1)
    pos = (n - 1 - step) if reverse else step

    @pl.when(step == 0)
    def _():
        carry_ref[...] = jnp.zeros(carry_ref.shape, F32)

    x = u_ref[:, 0:RG_W]
    has_prev = pos > 0
    has_next = pos < n - 1
    p6 = jnp.where(has_prev, up_ref[SUBLANES - 2:SUBLANES - 1, 0:RG_W], 0.0)
    p7 = jnp.where(has_prev, up_ref[SUBLANES - 1:SUBLANES, 0:RG_W], 0.0)
    nx = jnp.where(has_next, un_ref[0:1, 0:RG_W], 0.0)
    xr = cb_ref[...] + _shift_down(x, [p6, p7]) * cw_ref[0:1, :]
    xr = xr + _shift_down(x, [p7]) * cw_ref[1:2, :]
    xr = xr + x * cw_ref[2:3, :]
    xr = xr + _shift_up(x, nx) * cw_ref[3:4, :]

    x_hi = xr.astype(BF16)
    x_lo = (xr - x_hi.astype(F32)).astype(BF16)
    wh = wh_ref[...]
    pre = (jnp.dot(x_hi, wh, preferred_element_type=F32)
           + jnp.dot(x_lo, wh, preferred_element_type=F32)
           + jnp.dot(x_hi, wl_ref[...], preferred_element_type=F32)) + gb_ref[...]
    r = _sigmoid(pre[:, 0:RG_W])
    ig = _sigmoid(pre[:, RG_W:2 * RG_W])
    nl = -lam_ref[...]
    softplus = jnp.maximum(nl, 0.0) + jnp.log1p(jnp.exp(-jnp.abs(nl)))
    a = jnp.exp(-RG_C * r * softplus)
    a_ref[...] = a
    b_ref[...] = jnp.sqrt(1.0 - a * a) * (ig * xr)

    n_groups = tb // SUBLANES

    def body(gi, carry):
        g = (n_groups - 1 - gi) if reverse else gi
        rows = pl.ds(pl.multiple_of(g * SUBLANES, SUBLANES), SUBLANES)
        a, b = _scan8(a_ref[rows, :], b_ref[rows, :], reverse)
        h = a * carry + b
        h_ref[rows, :] = h
        return h[0:1, :] if reverse else h[SUBLANES - 1:SUBLANES, :]

    carry_ref[...] = lax.fori_loop(0, n_groups, body, carry_ref[...], unroll=RG_UNROLL)

    if combine:
        gate = u_ref[:, RG_W:2 * RG_W]
        gelu = 0.5 * gate * (1.0 + jnp.tanh(math.sqrt(2.0 / math.pi)
                                             * (gate + 0.044715 * (gate * gate * gate))))
        y = (hf_ref[...] + h_ref[...]) * gelu
        o_ref[...] = _rms(y) * gain_ref[...]


def _blockdiag(w):
    h, d, _ = w.shape
    eye = jnp.eye(h, dtype=w.dtype)
    return (eye[:, None, :, None] * w[:, :, None, :]).reshape(h * d, h * d)


def _rg_direction(u_rg, p, l, d, h_fwd=None, gain=None):
    b, seq_len, _ = u_rg.shape
    tb = min(SEQ_TILE, seq_len)
    nt = seq_len // tb
    reverse = d == 1
    combine = h_fwd is not None
    main, prev, nxt = _seq_specs(tb, RG_IN, nt, reverse)
    w = jnp.concatenate([_blockdiag(p['rg_wa'][l, d]), _blockdiag(p['rg_wx'][l, d])], axis=1)
    w_hi = w.astype(BF16)
    w_lo = (w - w_hi.astype(F32)).astype(BF16)
    gb = jnp.concatenate([p['rg_ba'][l, d], p['rg_bx'][l, d]]).reshape(1, 2 * RG_W)
    const = lambda bb, i: (0, 0)
    vec = pl.BlockSpec((1, RG_W), const)
    out = pl.BlockSpec((None, tb, RG_W), lambda bb, i: (bb, (nt - 1 - i) if reverse else i, 0))
    in_specs = [main, prev, nxt, pl.BlockSpec((4, RG_W), const), vec,
                pl.BlockSpec((RG_W, 2 * RG_W), const), pl.BlockSpec((RG_W, 2 * RG_W), const),
                pl.BlockSpec((1, 2 * RG_W), const), vec]
    args = [u_rg, u_rg, u_rg, p['rg_conv_w'][l], p['rg_conv_b'][l].reshape(1, RG_W), w_hi, w_lo, gb,
            p['rg_lambda'][l, d].reshape(1, RG_W)]
    if combine:
        in_specs += [out, vec]
        args += [h_fwd, gain.reshape(1, RG_W)]
    return pl.pallas_call(
        functools.partial(_rg_kernel, reverse=reverse, combine=combine, tb=tb),
        grid=(b, nt),
        in_specs=in_specs,
        out_specs=out,
        out_shape=jax.ShapeDtypeStruct((b, seq_len, RG_W), F32),
        scratch_shapes=[pltpu.VMEM((tb, RG_W), F32), pltpu.VMEM((tb, RG_W), F32),
                        pltpu.VMEM((1, RG_W), F32)],
        compiler_params=_params(2),
        name="rg_bwd" if reverse else "rg_fwd",
    )(*args)


def _rglru(u_rg, p, l, gain):
    b, seq_len, _ = u_rg.shape
    h_f = _rg_direction(u_rg, p, l, 0)
    y = _rg_direction(u_rg, p, l, 1, h_fwd=h_f, gain=gain)
    return y.reshape(b * seq_len, RG_W)


def _outffn_kernel(*refs, final, ff_chunks):
    if final:
        (x_ref, yhy_ref, yhg_ref, yrg_ref, g1_ref, sh2_ref, sc2_ref, g2_ref, n2_ref,
         wo_ref, wg_ref, wu_ref, wd_ref, fn_ref, o_ref) = refs
    else:
        (x_ref, yhy_ref, yhg_ref, yrg_ref, g1_ref, sh2_ref, sc2_ref, g2_ref, n2_ref,
         wo_ref, wg_ref, wu_ref, wd_ref, o_ref) = refs
    mix = jnp.dot(yhy_ref[...].astype(BF16), wo_ref[0:HY_W, :], preferred_element_type=F32)
    mix = mix + jnp.dot(yhg_ref[...].astype(BF16), wo_ref[HY_W:HY_W + HG_W, :],
                        preferred_element_type=F32)
    mix = mix + jnp.dot(yrg_ref[...].astype(BF16), wo_ref[HY_W + HG_W:D_MODEL, :],
                        preferred_element_type=F32)
    x1 = x_ref[...] + g1_ref[...] * mix
    h = _rms(x1) * n2_ref[...]
    h = (h * (1.0 + sc2_ref[...]) + sh2_ref[...]).astype(BF16)
    step = D_FF // ff_chunks
    ff = None
    for c in range(ff_chunks):
        sl = slice(c * step, (c + 1) * step)
        gate = jnp.dot(h, wg_ref[:, sl], preferred_element_type=F32)
        up = jnp.dot(h, wu_ref[:, sl], preferred_element_type=F32)
        act = ((gate * _sigmoid(gate)) * up).astype(BF16)
        part = jnp.dot(act, wd_ref[sl, :], preferred_element_type=F32)
        ff = part if ff is None else ff + part
    x2 = x1 + g2_ref[...] * ff
    if final:
        x2 = _rms(x2) * fn_ref[...]
    o_ref[...] = x2


def _outffn(x2, y_hy, y_hg, y_rg, mod4, norm_w, w_out, wg, wu, wd, seq_len, final_w=None):
    t = x2.shape[0]
    tm = min(TOKEN_TILE, seq_len)
    tpb = seq_len // tm
    final = final_w is not None
    row = lambda i: (i, 0)
    const = lambda i: (0, 0)

    def resident(shape):
        return pl.BlockSpec(shape, const, pipeline_mode=pl.Buffered(1))

    in_specs = [pl.BlockSpec((tm, D_MODEL), row), pl.BlockSpec((tm, HY_W), row),
                pl.BlockSpec((tm, HG_W), row), pl.BlockSpec((tm, RG_W), row),
                _mod_spec(2, tpb), _mod_spec(3, tpb), _mod_spec(4, tpb), _mod_spec(5, tpb),
                pl.BlockSpec((1, D_MODEL), const),
                resident((D_MODEL, D_MODEL)), resident((D_MODEL, D_FF)), resident((D_MODEL, D_FF)),
                resident((D_FF, D_MODEL))]
    args = [x2, y_hy, y_hg, y_rg, mod4, mod4, mod4, mod4, norm_w.reshape(1, D_MODEL),
            w_out, wg, wu, wd]
    if final:
        in_specs.append(pl.BlockSpec((1, D_MODEL), const))
        args.append(final_w.reshape(1, D_MODEL))
    return pl.pallas_call(
        functools.partial(_outffn_kernel, final=final, ff_chunks=2),
        grid=(t // tm,),
        in_specs=in_specs,
        out_specs=pl.BlockSpec((tm, D_MODEL), row),
        out_shape=jax.ShapeDtypeStruct((t, D_MODEL), F32),
        compiler_params=_params(1),
        name="outffn",
    )(*args)


def _trunk(x, mods, lb_all, p, wb):
    b, seq_len, _ = x.shape
    x2 = x.reshape(b * seq_len, D_MODEL)
    tables = _dft_tables(seq_len)
    for l in range(DEPTH):
        mod4 = mods[l].reshape(b, 6, 1, D_MODEL)
        gain = p['out_norm_w'][l]
        u_hy, u_hg, u_rg = _inproj(x2, mod4, p['norm1_w'][l], wb['w_in'][l], seq_len)
        y_hy = _hyena(u_hy.reshape(b, seq_len, HY_IN), p, l, gain[0:HY_W], tables)
        y_hg = _hgrn2(u_hg.reshape(b, seq_len, HG_IN), lb_all[0, l], lb_all[1, l],
                      gain[HY_W:HY_W + HG_W])
        y_rg = _rglru(u_rg.reshape(b, seq_len, RG_IN), p, l, gain[HY_W + HG_W:])
        x2 = _outffn(x2, y_hy, y_hg, y_rg, mod4, p['norm2_w'][l], wb['w_out'][l], wb['ffn_wg'][l],
                     wb['ffn_wu'][l], wb['ffn_wd'][l], seq_len,
                     final_w=p['final_norm_w'] if l == DEPTH - 1 else None)
    return x2.reshape(b, seq_len, D_MODEL)


def kernel(x_prompt, x_sample, c_prompt, c_sample, w_in, w_out, out_norm_w, ada_w, ada_b, norm1_w, norm2_w, final_norm_w, hy_conv_w, hy_conv_b, hy_w1, hy_b1, hy_w2, hy_b2, hy_w3, hy_b3, hy_w4, hy_freq, hy_bias, hg_lb_logits, rg_conv_w, rg_conv_b, rg_wa, rg_ba, rg_wx, rg_bx, rg_lambda, ffn_wg, ffn_wu, ffn_wd):
    p = dict(out_norm_w=out_norm_w, norm1_w=norm1_w, norm2_w=norm2_w, final_norm_w=final_norm_w,
             hy_conv_w=hy_conv_w, hy_conv_b=hy_conv_b, hy_w1=hy_w1, hy_b1=hy_b1, hy_w2=hy_w2,
             hy_b2=hy_b2, hy_w3=hy_w3, hy_b3=hy_b3, hy_w4=hy_w4, hy_freq=hy_freq, hy_bias=hy_bias,
             rg_conv_w=rg_conv_w, rg_conv_b=rg_conv_b, rg_wa=rg_wa, rg_ba=rg_ba, rg_wx=rg_wx,
             rg_bx=rg_bx, rg_lambda=rg_lambda)
    wb = dict(w_in=w_in.astype(BF16), w_out=w_out.astype(BF16), ffn_wg=ffn_wg.astype(BF16),
              ffn_wu=ffn_wu.astype(BF16), ffn_wd=ffn_wd.astype(BF16))
    lb_all = jax.nn.softmax(hg_lb_logits.astype(F32), axis=1)
    lb_all = jnp.cumsum(lb_all, axis=1) - lb_all[:, :1]
    nb_p, nb_s = c_prompt.shape[0], c_sample.shape[0]
    rows = -(-(nb_p + nb_s) // SUBLANES) * SUBLANES
    c_all = jnp.pad(jnp.concatenate([c_prompt, c_sample]), ((0, rows - nb_p - nb_s), (0, 0)))
    mods = _ada(c_all, ada_w, ada_b)
    y_prompt = _trunk(x_prompt, mods[:, 0:nb_p], lb_all, p, wb)
    y_sample = _trunk(x_sample, mods[:, nb_p:nb_p + nb_s], lb_all, p, wb)
    return (y_prompt, y_sample)
```

```python
import functools
import math

import jax
import jax.numpy as jnp
from jax import lax
from jax.experimental import pallas as pl
from jax.experimental.pallas import tpu as pltpu

F32 = jnp.float32
BF16 = jnp.bfloat16
HIGHEST = lax.Precision.HIGHEST

D_MODEL = 1024
DEPTH = 2
HY_W = 256
HG_W = 512
RG_W = 256
HY_IN = 3 * HY_W
HG_IN = 5 * HG_W
RG_IN = 2 * RG_W
D_IN = HY_IN + HG_IN + RG_IN
HY_EMB = 33
HY_BANDS = 16
HY_FFN = 64
HY_FAST_DECAY = 0.3
HY_SLOW_DECAY = 1.5
HY_TARGET = 1e-2
HG_HEADS = 4
HG_D = 128
RG_HEADS = 4
RG_HD = 64
RG_C = 8.0
D_FF = 2816
EPS = 1e-6
LOG2E = 1.4426950408889634

LANES = 128
SUBLANES = 8
VMEM_LIMIT = 56 * 1024 * 1024

TOKEN_TILE = 512
SEQ_TILE = 512
HG_CHUNK = 64
HG_SUB = 32
HG_UNROLL = 8
RG_UNROLL = 8
DFT_N1 = 128
DFT_ROWS = 2048
DFT_GROUPS = 2


def _params(n_axes):
    return pltpu.CompilerParams(dimension_semantics=("arbitrary",) * n_axes,
                                vmem_limit_bytes=VMEM_LIMIT)


def _dot_bf16(a, b):
    return jnp.dot(a.astype(BF16), b.astype(BF16), preferred_element_type=F32)


def _dot_f32(a, b):
    return jnp.dot(a, b, preferred_element_type=F32, precision=HIGHEST)


def _dot_nt(a, b):
    return lax.dot_general(a.astype(BF16), b.astype(BF16), (((1,), (1,)), ((), ())),
                           preferred_element_type=F32)


def _dot_tn(a, b):
    return lax.dot_general(a.astype(BF16), b.astype(BF16), (((0,), (0,)), ((), ())),
                           preferred_element_type=F32)


def _rms(x):
    return x * lax.rsqrt(jnp.mean(x * x, axis=-1, keepdims=True) + EPS)


def _sigmoid(x):
    return 1.0 / (1.0 + jnp.exp(-x))


def _row_iota(shape):
    return lax.broadcasted_iota(jnp.int32, shape, 0)


def _shift_down(x, fill_rows):
    s = len(fill_rows)
    y = pltpu.roll(x, s, 0)
    head = y[0:SUBLANES]
    row = _row_iota(head.shape)
    for j, r in enumerate(fill_rows):
        head = jnp.where(row == j, r, head)
    return jnp.concatenate([head, y[SUBLANES:]], axis=0)


def _shift_up(x, next_row):
    n = x.shape[0]
    y = pltpu.roll(x, n - 1, 0)
    tail = y[n - SUBLANES:n]
    tail = jnp.where(_row_iota(tail.shape) == SUBLANES - 1, next_row, tail)
    return jnp.concatenate([y[0:n - SUBLANES], tail], axis=0)


def _ada_kernel(c_ref, w_ref, b_ref, o_ref):
    c = c_ref[...]
    s = c * _sigmoid(c)
    o_ref[0] = _dot_f32(s, w_ref[0]) + b_ref[0]


def _ada(c_all, ada_w, ada_b):
    rows = c_all.shape[0]
    n_out = ada_w.shape[-1]
    tn = n_out // 4
    return pl.pallas_call(
        _ada_kernel,
        grid=(DEPTH, n_out // tn),
        in_specs=[pl.BlockSpec((rows, D_MODEL), lambda l, j: (0, 0)),
                  pl.BlockSpec((1, D_MODEL, tn), lambda l, j: (l, 0, j)),
                  pl.BlockSpec((1, 1, tn), lambda l, j: (l, 0, j))],
        out_specs=pl.BlockSpec((1, rows, tn), lambda l, j: (l, 0, j)),
        out_shape=jax.ShapeDtypeStruct((DEPTH, rows, n_out), F32),
        compiler_params=_params(2),
        name="ada",
    )(c_all, ada_w, ada_b.reshape(DEPTH, 1, n_out))


def _mod_spec(j, tiles_per_batch):
    return pl.BlockSpec((None, None, 1, D_MODEL), lambda i: (i // tiles_per_batch, j, 0, 0))


def _inproj_kernel(x_ref, xp_ref, xn_ref, nw_ref, sh_ref, sc_ref, w_ref, cw_ref, cb_ref,
                   ov_ref, ox0_ref, ohg_ref, org_ref, *, tpb):
    tm = x_ref.shape[0]
    pos = pl.program_id(0) % tpb
    xe = jnp.concatenate([xp_ref[...], x_ref[...], xn_ref[...]], axis=0)
    he = _rms(xe) * nw_ref[...]
    he = he * (1.0 + sc_ref[...]) + sh_ref[...]
    h = he[SUBLANES:tm + SUBLANES].astype(BF16)
    ohg_ref[...] = jnp.dot(h, w_ref[:, HY_IN:HY_IN + HG_IN], preferred_element_type=F32)
    org_ref[...] = jnp.dot(h, w_ref[:, HY_IN + HG_IN:D_IN], preferred_element_type=F32)
    ue = jnp.dot(he.astype(BF16), w_ref[:, 0:HY_IN], preferred_element_type=F32)
    x = ue[SUBLANES:tm + SUBLANES]
    prev = jnp.where(pos > 0, ue[SUBLANES - 1:SUBLANES], 0.0)
    nxt = jnp.where(pos < tpb - 1, ue[tm + SUBLANES:tm + SUBLANES + 1], 0.0)
    uc = cb_ref[...] + _shift_down(x, [prev]) * cw_ref[0:1, :]
    uc = uc + x * cw_ref[1:2, :]
    uc = uc + _shift_up(x, nxt) * cw_ref[2:3, :]
    ox0_ref[...] = uc[:, 0:HY_W]
    ov_ref[...] = uc[:, 2 * HY_W:3 * HY_W] * uc[:, HY_W:2 * HY_W]


def _inproj(x2, mod4, norm_w, w_in_bf16, seq_len, conv_w, conv_b):
    t = x2.shape[0]
    tm = min(TOKEN_TILE, seq_len)
    tpb = seq_len // tm
    r8 = tm // SUBLANES
    last8 = t // SUBLANES - 1
    row = lambda i: (i, 0)
    const = lambda i: (0, 0)
    return pl.pallas_call(
        functools.partial(_inproj_kernel, tpb=tpb),
        grid=(t // tm,),
        in_specs=[pl.BlockSpec((tm, D_MODEL), row),
                  pl.BlockSpec((SUBLANES, D_MODEL), lambda i: (jnp.maximum(i * r8 - 1, 0), 0)),
                  pl.BlockSpec((SUBLANES, D_MODEL), lambda i: (jnp.minimum((i + 1) * r8, last8), 0)),
                  pl.BlockSpec((1, D_MODEL), const),
                  _mod_spec(0, tpb), _mod_spec(1, tpb),
                  pl.BlockSpec((D_MODEL, D_IN), const),
                  pl.BlockSpec((3, HY_IN), const), pl.BlockSpec((1, HY_IN), const)],
        out_specs=[pl.BlockSpec((tm, HY_W), row), pl.BlockSpec((tm, HY_W), row),
                   pl.BlockSpec((tm, HG_IN), row), pl.BlockSpec((tm, RG_IN), row)],
        out_shape=[jax.ShapeDtypeStruct((t, HY_W), F32), jax.ShapeDtypeStruct((t, HY_W), F32),
                   jax.ShapeDtypeStruct((t, HG_IN), F32), jax.ShapeDtypeStruct((t, RG_IN), F32)],
        compiler_params=_params(1),
        name="inproj",
    )(x2, x2, x2, norm_w.reshape(1, D_MODEL), mod4, mod4, w_in_bf16, conv_w,
      conv_b.reshape(1, HY_IN))


def _seq_specs(tb, width, n_tiles, reverse=False):
    r8 = tb // SUBLANES
    last8 = n_tiles * r8 - 1

    def pos(i):
        return (n_tiles - 1 - i) if reverse else i

    main = pl.BlockSpec((None, tb, width), lambda b, i: (b, pos(i), 0))
    prev = pl.BlockSpec((None, SUBLANES, width),
                        lambda b, i: (b, jnp.maximum(pos(i) * r8 - 1, 0), 0))
    nxt = pl.BlockSpec((None, SUBLANES, width),
                       lambda b, i: (b, jnp.minimum((pos(i) + 1) * r8, last8), 0))
    return main, prev, nxt


def _hy_pre_kernel(u_ref, up_ref, un_ref, cw_ref, cb_ref, v_ref, x0_ref):
    i = pl.program_id(1)
    n = pl.num_programs(1)
    x = u_ref[...]
    prev = jnp.where(i > 0, up_ref[SUBLANES - 1:SUBLANES, :], 0.0)
    nxt = jnp.where(i < n - 1, un_ref[0:1, :], 0.0)
    uc = cb_ref[...] + _shift_down(x, [prev]) * cw_ref[0:1, :]
    uc = uc + x * cw_ref[1:2, :]
    uc = uc + _shift_up(x, nxt) * cw_ref[2:3, :]
    x0_ref[...] = uc[:, 0:HY_W]
    v_ref[...] = uc[:, 2 * HY_W:3 * HY_W] * uc[:, HY_W:2 * HY_W]


def _hy_pre(u_hy, conv_w, conv_b):
    b, l, _ = u_hy.shape
    tb = min(SEQ_TILE, l)
    nt = l // tb
    main, prev, nxt = _seq_specs(tb, HY_IN, nt)
    out = pl.BlockSpec((None, tb, HY_W), lambda bb, i: (bb, i, 0))
    return pl.pallas_call(
        _hy_pre_kernel,
        grid=(b, nt),
        in_specs=[main, prev, nxt,
                  pl.BlockSpec((3, HY_IN), lambda bb, i: (0, 0)),
                  pl.BlockSpec((1, HY_IN), lambda bb, i: (0, 0))],
        out_specs=[out, out],
        out_shape=[jax.ShapeDtypeStruct((b, l, HY_W), F32)] * 2,
        compiler_params=_params(2),
        name="hy_pre",
    )(u_hy, u_hy, u_hy, conv_w, conv_b.reshape(1, HY_IN))


def _hy_filter_kernel(fb_ref, w1t_ref, w1f_ref, b1_ref, w2_ref, b2_ref, w3_ref, b3_ref, w4_ref,
                      fr_ref, dl_ref, k_ref, *, seq_len, tile):
    half = tile // 2
    base = pl.program_id(0) * tile
    lane = lax.broadcasted_iota(jnp.int32, (half, LANES), 1)
    n = (_row_iota((half, LANES)) + base + jnp.where(lane >= HY_FFN, half, 0)).astype(F32)
    t = n / float(seq_len - 1)
    ang = fb_ref[...] * n * (2.0 * math.pi / seq_len)
    feat = jnp.where((lane & (HY_FFN - 1)) < HY_BANDS, jnp.cos(ang), -jnp.sin(ang))
    fr = fr_ref[...]
    h = jnp.sin(fr * (t * w1t_ref[...] + _dot_f32(feat, w1f_ref[...]) + b1_ref[...]))
    h = jnp.sin(fr * (_dot_f32(h, w2_ref[...]) + b2_ref[...]))
    h = jnp.sin(fr * (_dot_f32(h, w3_ref[...]) + b3_ref[...]))
    k = jnp.concatenate([_dot_f32(h, w4_ref[0]), _dot_f32(h, w4_ref[1])], axis=0)
    n2 = (_row_iota((tile, HY_W)) + base)
    decay = jnp.exp(-(n2.astype(F32) / float(seq_len - 1)) * dl_ref[...])
    k_ref[0] = k[:, 0:HY_W] * decay
    k_ref[1] = jnp.where(n2 == 0, 0.0, k[:, HY_W:2 * HY_W] * decay)


def _pad2(a, rows, cols):
    return jnp.pad(a, ((0, rows - a.shape[0]), (0, cols - a.shape[1])))


def _hy_filter(seq_len, w1, b1, w2, b2, w3, b3, w4, freq):
    tile = min(SEQ_TILE, seq_len)
    fb = jnp.linspace(1e-4, HY_BANDS - 1, HY_BANDS, dtype=F32)
    deltas = jnp.abs(jnp.linspace(math.log(HY_TARGET) / HY_SLOW_DECAY,
                                  math.log(HY_TARGET) / HY_FAST_DECAY, HY_W, dtype=F32))[None, :]
    zeros = jnp.zeros((HY_FFN, HY_FFN), F32)
    twice = lambda a: jnp.concatenate([a, a])[None, :]
    diag2 = lambda a: jnp.concatenate([jnp.concatenate([a, zeros], 1),
                                       jnp.concatenate([zeros, a], 1)], 0)
    w4z = jnp.zeros_like(w4)
    args = (twice(_pad2(jnp.concatenate([fb, fb])[None, :], 1, HY_FFN)[0]), twice(w1[0]),
            diag2(_pad2(w1[1:], HY_FFN, HY_FFN)), twice(b1), diag2(w2), twice(b2), diag2(w3), twice(b3),
            jnp.stack([jnp.concatenate([w4, w4z], 0), jnp.concatenate([w4z, w4], 0)]), twice(freq),
            deltas)
    const = lambda i: (0,) * 2
    return pl.pallas_call(
        functools.partial(_hy_filter_kernel, seq_len=seq_len, tile=tile),
        grid=(seq_len // tile,),
        in_specs=[pl.BlockSpec(a.shape, (lambda i: (0, 0, 0)) if a.ndim == 3 else const)
                  for a in args],
        out_specs=pl.BlockSpec((2, tile, HY_W), lambda i: (0, i, 0)),
        out_shape=jax.ShapeDtypeStruct((2, seq_len, HY_W), F32),
        compiler_params=_params(1),
        name="hy_filter",
    )(*args)


def _dft_tables(seq_len):
    n = 2 * seq_len
    n1 = DFT_N1
    n2 = n // n1
    grp = jnp.arange(n2 // SUBLANES, dtype=jnp.int32)[:, None, None, None]
    k1 = jnp.arange(n1, dtype=jnp.int32)[None, :, None, None]
    g = jnp.arange(SUBLANES, dtype=jnp.int32)[None, None, :, None]
    t1 = jnp.arange(n1 // 2, dtype=jnp.int32)[None, None, None, :]
    th = ((k1 * (n2 * t1 + SUBLANES * grp + g)) % n).astype(F32) * (2.0 * math.pi / n)
    trig = jnp.concatenate([jnp.cos(th), -jnp.sin(th)], axis=1)
    groups = n2 // SUBLANES
    fwd = trig.astype(BF16).reshape(groups, 2 * n1 * SUBLANES, n1 // 2)
    inv = jnp.transpose((trig * (1.0 / n)).astype(BF16), (0, 3, 2, 1))
    inv = inv.reshape(groups, (n1 // 2) * SUBLANES, 2 * n1)
    a = jnp.arange(n2, dtype=jnp.int32)
    ph = ((a[:, None] * a[None, :]) % n2).astype(F32) * (2.0 * math.pi / n2)
    c, s = jnp.cos(ph), jnp.sin(ph)
    g_fwd = jnp.concatenate([jnp.concatenate([c, s], 1), jnp.concatenate([-s, c], 1)], 0)
    g_inv = jnp.concatenate([jnp.concatenate([c, -s], 1), jnp.concatenate([s, c], 1)], 0)
    return dict(fwd=fwd, inv=inv, g_fwd=g_fwd.astype(BF16), g_inv=g_inv.astype(BF16))


def _expand_cols(t):
    rows, k = t.shape
    wide = k * SUBLANES
    rep = (lax.broadcasted_iota(jnp.int32, (k, wide), 1) // SUBLANES
           == lax.broadcasted_iota(jnp.int32, (k, wide), 0))
    e = jnp.dot(t, jnp.where(rep, 1.0, 0.0).astype(BF16), preferred_element_type=F32)
    keep = ((_row_iota((rows, wide)) & (SUBLANES - 1))
            == (lax.broadcasted_iota(jnp.int32, (rows, wide), 1) & (SUBLANES - 1)))
    return jnp.where(keep, e, 0.0).astype(BF16)


def _dot_table(table, x):
    return jnp.dot(table, x.astype(BF16), preferred_element_type=F32)


def _dft_a_kernel(x_ref, t_ref, y_ref, e_ref):
    @pl.when(pl.program_id(1) == 0)
    def _():
        for s in range(DFT_GROUPS):
            e_ref[s] = _expand_cols(t_ref[s])

    half, _, width = x_ref.shape
    parts = []
    for s in range(DFT_GROUPS):
        xs = x_ref[:, s * SUBLANES:(s + 1) * SUBLANES, :].reshape(half * SUBLANES, width)
        parts.append(_dot_table(e_ref[s], xs).reshape(4 * half, SUBLANES, width))
    y_ref[...] = jnp.concatenate(parts, axis=1).astype(BF16)


def _dft_a(x, table):
    nb, half, n2, _ = x.shape
    cols = DFT_GROUPS * SUBLANES
    return pl.pallas_call(
        _dft_a_kernel,
        grid=(n2 // cols, nb),
        in_specs=[pl.BlockSpec((None, half, cols, HY_W), lambda j, b: (b, 0, j, 0)),
                  pl.BlockSpec((DFT_GROUPS, 4 * half * SUBLANES, half), lambda j, b: (j, 0, 0))],
        out_specs=pl.BlockSpec((None, 4 * half, cols, HY_W), lambda j, b: (b, 0, j, 0)),
        out_shape=jax.ShapeDtypeStruct((nb, 4 * half, n2, HY_W), BF16),
        scratch_shapes=[pltpu.VMEM((DFT_GROUPS, 4 * half * SUBLANES, half * SUBLANES), BF16)],
        compiler_params=_params(2),
        name="hy_dft_a",
    )(x, table)


def _dft_b_filter_kernel(y_ref, g_ref, k_ref, *, groups):
    g = g_ref[...]
    n2 = y_ref.shape[-2]
    for j in range(groups):
        xf = _dot_table(g, jnp.concatenate([y_ref[0, 0, j], y_ref[0, 1, j]], axis=0))
        xb = _dot_table(g, jnp.concatenate([y_ref[1, 0, j], y_ref[1, 1, j]], axis=0))
        k_ref[0, j] = xf[0:n2] + xb[0:n2]
        k_ref[1, j] = xf[n2:2 * n2] - xb[n2:2 * n2]


def _dft_b_filter(y, g_fwd):
    _, _, n1, n2, _ = y.shape
    groups = max(1, DFT_ROWS // n2)
    const = lambda i: (0, 0)
    return pl.pallas_call(
        functools.partial(_dft_b_filter_kernel, groups=groups),
        grid=(n1 // groups,),
        in_specs=[pl.BlockSpec((2, 2, groups, n2, HY_W), lambda i: (0, 0, i, 0, 0)),
                  pl.BlockSpec((2 * n2, 2 * n2), const)],
        out_specs=pl.BlockSpec((2, groups, n2, HY_W), lambda i: (0, i, 0, 0)),
        out_shape=jax.ShapeDtypeStruct((2, n1, n2, HY_W), F32),
        compiler_params=_params(1),
        name="hy_dft_b_filter",
    )(y, g_fwd)


def _dft_b_kernel(y_ref, k_ref, gf_ref, gi_ref, u_ref, *, groups):
    gf = gf_ref[...]
    gi = gi_ref[...]
    n2 = y_ref.shape[-2]
    for j in range(groups):
        x = _dot_table(gf, jnp.concatenate([y_ref[0, j], y_ref[1, j]], axis=0))
        xr, xi = x[0:n2], x[n2:2 * n2]
        kr, ki = k_ref[0, j], k_ref[1, j]
        z = jnp.concatenate([xr * kr - xi * ki, xr * ki + xi * kr], axis=0)
        u = _dot_table(gi, z).astype(BF16)
        u_ref[0, j] = u[0:n2]
        u_ref[1, j] = u[n2:2 * n2]


def _dft_b(y, kspec, g_fwd, g_inv):
    nb, _, n1, n2, _ = y.shape
    groups = max(1, DFT_ROWS // n2)
    const = lambda i, b: (0, 0)
    return pl.pallas_call(
        functools.partial(_dft_b_kernel, groups=groups),
        grid=(n1 // groups, nb),
        in_specs=[pl.BlockSpec((None, 2, groups, n2, HY_W), lambda i, b: (b, 0, i, 0, 0)),
                  pl.BlockSpec((2, groups, n2, HY_W), lambda i, b: (0, i, 0, 0)),
                  pl.BlockSpec((2 * n2, 2 * n2), const), pl.BlockSpec((2 * n2, 2 * n2), const)],
        out_specs=pl.BlockSpec((None, 2, groups, n2, HY_W), lambda i, b: (b, 0, i, 0, 0)),
        out_shape=jax.ShapeDtypeStruct(y.shape, BF16),
        compiler_params=_params(2),
        name="hy_dft_b",
    )(y, kspec, g_fwd, g_inv)


def _dft_a_inv_kernel(u_ref, t_ref, v_ref, x0_ref, bias_ref, gain_ref, o_ref, e_ref):
    @pl.when(pl.program_id(1) == 0)
    def _():
        for s in range(DFT_GROUPS):
            e_ref[s] = _expand_cols(t_ref[s])

    half, _, width = v_ref.shape
    rows = half * SUBLANES
    u = u_ref[...].astype(F32)
    for s in range(DFT_GROUPS):
        sl = slice(s * SUBLANES, (s + 1) * SUBLANES)
        y = _dot_table(e_ref[s], u[:, sl, :].reshape(4 * rows, width))
        y = ((y + v_ref[:, sl, :].reshape(rows, width) * bias_ref[...])
             * x0_ref[:, sl, :].reshape(rows, width))
        o_ref[:, sl, :] = (_rms(y) * gain_ref[...]).reshape(half, SUBLANES, width)


def _dft_a_inv(u, table, v, x0, bias, gain):
    nb, half, n2, _ = v.shape
    grp = DFT_GROUPS * SUBLANES
    tile = pl.BlockSpec((None, half, grp, HY_W), lambda j, b: (b, 0, j, 0))
    vec = pl.BlockSpec((1, HY_W), lambda j, b: (0, 0))
    return pl.pallas_call(
        _dft_a_inv_kernel,
        grid=(n2 // grp, nb),
        in_specs=[pl.BlockSpec((None, 4 * half, grp, HY_W), lambda j, b: (b, 0, j, 0)),
                  pl.BlockSpec((DFT_GROUPS, half * SUBLANES, 4 * half), lambda j, b: (j, 0, 0)),
                  tile, tile, vec, vec],
        scratch_shapes=[pltpu.VMEM((DFT_GROUPS, half * SUBLANES, 4 * half * SUBLANES), BF16)],
        out_specs=tile,
        out_shape=jax.ShapeDtypeStruct(v.shape, F32),
        compiler_params=_params(2),
        name="hy_dft_a_inv",
    )(u, table, v, x0, bias, gain)


def _hyena(v, x0, p, l, gain, tables):
    b, seq_len, _ = v.shape
    n1 = DFT_N1
    n2 = 2 * seq_len // n1
    kk = _hy_filter(seq_len, p['hy_w1'][l], p['hy_b1'][l], p['hy_w2'][l], p['hy_b2'][l],
                    p['hy_w3'][l], p['hy_b3'][l], p['hy_w4'][l], p['hy_freq'][l])
    ky = _dft_a(kk.reshape(2, n1 // 2, n2, HY_W), tables['fwd'])
    kspec = _dft_b_filter(ky.reshape(2, 2, n1, n2, HY_W), tables['g_fwd'])
    vm = v.reshape(b, n1 // 2, n2, HY_W)
    y = _dft_a(vm, tables['fwd']).reshape(b, 2, n1, n2, HY_W)
    u = _dft_b(y, kspec, tables['g_fwd'], tables['g_inv']).reshape(b, 2 * n1, n2, HY_W)
    out = _dft_a_inv(u, tables['inv'], vm, x0.reshape(b, n1 // 2, n2, HY_W),
                     p['hy_bias'][l].reshape(1, HY_W), gain.reshape(1, HY_W))
    return out.reshape(b * seq_len, HY_W)


def _cumsum_rows(g):
    n = g.shape[0]
    row = _row_iota((SUBLANES, g.shape[1]))
    s = 1
    while s < n:
        if s < SUBLANES:
            y = pltpu.roll(g, s, 0)
            head = jnp.where(row >= s, y[0:SUBLANES], 0.0)
            g = g + jnp.concatenate([head, y[SUBLANES:]], axis=0)
        else:
            g = jnp.concatenate([g[0:s], g[s:] + g[0:n - s]], axis=0)
        s *= 2
    return g


def _hg_chunk(q, v, z, lb, st, reverse):
    c = q.shape[0]
    e = jnp.exp2(z * (-LOG2E))
    s = 1.0 / (1.0 + e)
    g = jnp.log2(lb + (1.0 - lb) * s)
    kk = (1.0 - lb) * (e * s)
    bq = _cumsum_rows(g)
    total = bq[c - 1:c, :]
    if reverse:
        bq = total - bq + g
    blocks = []
    for i in range(c // HG_SUB):
        lo, hi = i * HG_SUB, (i + 1) * HG_SUB
        ks, ke = (lo, c) if reverse else (0, hi)
        ref = bq[lo + HG_SUB // 2:lo + HG_SUB // 2 + 1, :]
        qt = q[lo:hi] * jnp.exp2(bq[lo:hi] - ref)
        kt = kk[ks:ke] * jnp.exp2(ref - bq[ks:ke])
        parts = ([jnp.zeros((ks, HG_D), F32)] if ks else []) + [kt]
        parts += [jnp.zeros((c - ke, HG_D), F32)] if ke < c else []
        blocks.append(_dot_nt(qt, jnp.concatenate(parts, axis=0)))
    a = jnp.concatenate(blocks, axis=0)
    r2 = _row_iota((c, c))
    c2 = lax.broadcasted_iota(jnp.int32, (c, c), 1)
    a = jnp.where((c2 >= r2) if reverse else (c2 <= r2), a, 0.0)
    o = _dot_nt(q * jnp.exp2(bq), st) + _dot_bf16(a, v)
    st_new = st * jnp.exp2(total) + _dot_tn(v, kk * jnp.exp2(total - bq))
    return o, st_new


def _hg_kernel(*refs, reverse, combine, tb):
    if combine:
        q_ref, v_ref, z_ref, lb_ref, of_ref, g_ref, gain_ref, o_ref, st_ref, ob_ref = refs
    else:
        q_ref, v_ref, z_ref, lb_ref, o_ref, st_ref = refs
        ob_ref = o_ref

    @pl.when(pl.program_id(1) == 0)
    def _():
        st_ref[...] = jnp.zeros(st_ref.shape, F32)

    n_chunks = tb // HG_CHUNK

    def body(ci, carry):
        c = (n_chunks - 1 - ci) if reverse else ci
        r0 = pl.multiple_of(c * HG_CHUNK, HG_CHUNK)
        rows = pl.ds(r0, HG_CHUNK)
        for h in range(HG_HEADS):
            cols = slice(h * HG_D, (h + 1) * HG_D)
            o, st = _hg_chunk(q_ref[rows, cols], v_ref[rows, cols], z_ref[rows, cols],
                              lb_ref[:, cols], st_ref[h], reverse)
            st_ref[h] = st
            ob_ref[rows, cols] = o
        return carry

    lax.fori_loop(0, n_chunks, body, 0, unroll=HG_UNROLL)

    if combine:
        gate = g_ref[...]
        gate = gate * _sigmoid(gate)
        for h in range(HG_HEADS):
            cols = slice(h * HG_D, (h + 1) * HG_D)
            o = of_ref[:, cols] + ob_ref[:, cols]
            o_ref[:, cols] = _rms(o) * gain_ref[:, cols] * gate[:, cols]


def _hg_direction(u_hg, lb, reverse, o_fwd=None, gain=None):
    b, l, _ = u_hg.shape
    tb = min(SEQ_TILE, l)
    nt = l // tb
    combine = o_fwd is not None

    def col(j):
        return pl.BlockSpec((None, tb, HG_W),
                            lambda bb, i: (bb, (nt - 1 - i) if reverse else i, j))

    vec = pl.BlockSpec((1, HG_W), lambda bb, i: (0, 0))
    in_specs = [col(0), col(1), col(3 if reverse else 2), vec]
    args = [u_hg, u_hg, u_hg, lb.reshape(1, HG_W)]
    scratch = [pltpu.VMEM((HG_HEADS, HG_D, HG_D), F32)]
    if combine:
        in_specs += [col(0), col(4), vec]
        args += [o_fwd, u_hg, gain.reshape(1, HG_W)]
        scratch += [pltpu.VMEM((tb, HG_W), F32)]
    return pl.pallas_call(
        functools.partial(_hg_kernel, reverse=reverse, combine=combine, tb=tb),
        grid=(b, nt),
        in_specs=in_specs,
        out_specs=col(0),
        out_shape=jax.ShapeDtypeStruct((b, l, HG_W), F32),
        scratch_shapes=scratch,
        compiler_params=_params(2),
        name="hg_bwd" if reverse else "hg_fwd",
    )(*args)


def _hgrn2(u_hg, lb_f, lb_b, gain):
    b, l, _ = u_hg.shape
    o_f = _hg_direction(u_hg, lb_f, reverse=False)
    y = _hg_direction(u_hg, lb_b, reverse=True, o_fwd=o_f, gain=gain)
    return y.reshape(b * l, HG_W)


def _scan8(a, b, reverse):
    row = _row_iota(a.shape)
    for s in (1, 2, 4):
        if reverse:
            keep = row < SUBLANES - s
            a_s, b_s = pltpu.roll(a, SUBLANES - s, 0), pltpu.roll(b, SUBLANES - s, 0)
        else:
            keep = row >= s
            a_s, b_s = pltpu.roll(a, s, 0), pltpu.roll(b, s, 0)
        b = jnp.where(keep, a * b_s, 0.0) + b
        a = jnp.where(keep, a * a_s, a)
    return a, b


def _rg_kernel(*refs, reverse, combine, tb):
    if combine:
        (u_ref, up_ref, un_ref, cw_ref, cb_ref, wh_ref, wl_ref, gb_ref, lam_ref,
         hf_ref, gain_ref, o_ref, a_ref, b_ref, carry_ref) = refs
        h_ref = b_ref
    else:
        (u_ref, up_ref, un_ref, cw_ref, cb_ref, wh_ref, wl_ref, gb_ref, lam_ref,
         o_ref, a_ref, b_ref, carry_ref) = refs
        h_ref = o_ref
    step = pl.program_id(1)
    n = pl.num_programs(1)
    pos = (n - 1 - step) if reverse else step

    @pl.when(step == 0)
    def _():
        carry_ref[...] = jnp.zeros(carry_ref.shape, F32)

    x = u_ref[:, 0:RG_W]
    has_prev = pos > 0
    has_next = pos < n - 1
    p6 = jnp.where(has_prev, up_ref[SUBLANES - 2:SUBLANES - 1, 0:RG_W], 0.0)
    p7 = jnp.where(has_prev, up_ref[SUBLANES - 1:SUBLANES, 0:RG_W], 0.0)
    nx = jnp.where(has_next, un_ref[0:1, 0:RG_W], 0.0)
    xr = cb_ref[...] + _shift_down(x, [p6, p7]) * cw_ref[0:1, :]
    xr = xr + _shift_down(x, [p7]) * cw_ref[1:2, :]
    xr = xr + x * cw_ref[2:3, :]
    xr = xr + _shift_up(x, nx) * cw_ref[3:4, :]

    x_hi = xr.astype(BF16)
    x_lo = (xr - x_hi.astype(F32)).astype(BF16)
    wh = wh_ref[...]
    pre = (jnp.dot(x_hi, wh, preferred_element_type=F32)
           + jnp.dot(x_lo, wh, preferred_element_type=F32)
           + jnp.dot(x_hi, wl_ref[...], preferred_element_type=F32)) + gb_ref[...]
    r = _sigmoid(pre[:, 0:RG_W])
    ig = _sigmoid(pre[:, RG_W:2 * RG_W])
    nl = -lam_ref[...]
    softplus = jnp.maximum(nl, 0.0) + jnp.log1p(jnp.exp(-jnp.abs(nl)))
    a = jnp.exp(-RG_C * r * softplus)
    a_ref[...] = a
    b_ref[...] = jnp.sqrt(1.0 - a * a) * (ig * xr)

    n_groups = tb // SUBLANES

    def body(gi, carry):
        g = (n_groups - 1 - gi) if reverse else gi
        rows = pl.ds(pl.multiple_of(g * SUBLANES, SUBLANES), SUBLANES)
        a, b = _scan8(a_ref[rows, :], b_ref[rows, :], reverse)
        h = a * carry + b
        h_ref[rows, :] = h
        return h[0:1, :] if reverse else h[SUBLANES - 1:SUBLANES, :]

    carry_ref[...] = lax.fori_loop(0, n_groups, body, carry_ref[...], unroll=RG_UNROLL)

    if combine:
        gate = u_ref[:, RG_W:2 * RG_W]
        gelu = 0.5 * gate * (1.0 + jnp.tanh(math.sqrt(2.0 / math.pi)
                                             * (gate + 0.044715 * (gate * gate * gate))))
        y = (hf_ref[...] + h_ref[...]) * gelu
        o_ref[...] = _rms(y) * gain_ref[...]


def _blockdiag(w):
    h, d, _ = w.shape
    eye = jnp.eye(h, dtype=w.dtype)
    return (eye[:, None, :, None] * w[:, :, None, :]).reshape(h * d, h * d)


def _rg_direction(u_rg, p, l, d, h_fwd=None, gain=None):
    b, seq_len, _ = u_rg.shape
    tb = min(SEQ_TILE, seq_len)
    nt = seq_len // tb
    reverse = d == 1
    combine = h_fwd is not None
    main, prev, nxt = _seq_specs(tb, RG_IN, nt, reverse)
    w = jnp.concatenate([_blockdiag(p['rg_wa'][l, d]), _blockdiag(p['rg_wx'][l, d])], axis=1)
    w_hi = w.astype(BF16)
    w_lo = (w - w_hi.astype(F32)).astype(BF16)
    gb = jnp.concatenate([p['rg_ba'][l, d], p['rg_bx'][l, d]]).reshape(1, 2 * RG_W)
    const = lambda bb, i: (0, 0)
    vec = pl.BlockSpec((1, RG_W), const)
    out = pl.BlockSpec((None, tb, RG_W), lambda bb, i: (bb, (nt - 1 - i) if reverse else i, 0))
    in_specs = [main, prev, nxt, pl.BlockSpec((4, RG_W), const), vec,
                pl.BlockSpec((RG_W, 2 * RG_W), const), pl.BlockSpec((RG_W, 2 * RG_W), const),
                pl.BlockSpec((1, 2 * RG_W), const), vec]
    args = [u_rg, u_rg, u_rg, p['rg_conv_w'][l], p['rg_conv_b'][l].reshape(1, RG_W), w_hi, w_lo, gb,
            p['rg_lambda'][l, d].reshape(1, RG_W)]
    if combine:
        in_specs += [out, vec]
        args += [h_fwd, gain.reshape(1, RG_W)]
    return pl.pallas_call(
        functools.partial(_rg_kernel, reverse=reverse, combine=combine, tb=tb),
        grid=(b, nt),
        in_specs=in_specs,
        out_specs=out,
        out_shape=jax.ShapeDtypeStruct((b, seq_len, RG_W), F32),
        scratch_shapes=[pltpu.VMEM((tb, RG_W), F32), pltpu.VMEM((tb, RG_W), F32),
                        pltpu.VMEM((1, RG_W), F32)],
        compiler_params=_params(2),
        name="rg_bwd" if reverse else "rg_fwd",
    )(*args)


def _rglru(u_rg, p, l, gain):
    b, seq_len, _ = u_rg.shape
    h_f = _rg_direction(u_rg, p, l, 0)
    y = _rg_direction(u_rg, p, l, 1, h_fwd=h_f, gain=gain)
    return y.reshape(b * seq_len, RG_W)


def _outffn_kernel(*refs, final, ff_chunks):
    if final:
        (x_ref, yhy_ref, yhg_ref, yrg_ref, g1_ref, sh2_ref, sc2_ref, g2_ref, n2_ref,
         wo_ref, wg_ref, wu_ref, wd_ref, fn_ref, o_ref) = refs
    else:
        (x_ref, yhy_ref, yhg_ref, yrg_ref, g1_ref, sh2_ref, sc2_ref, g2_ref, n2_ref,
         wo_ref, wg_ref, wu_ref, wd_ref, o_ref) = refs
    mix = jnp.dot(yhy_ref[...].astype(BF16), wo_ref[0:HY_W, :], preferred_element_type=F32)
    mix = mix + jnp.dot(yhg_ref[...].astype(BF16), wo_ref[HY_W:HY_W + HG_W, :],
                        preferred_element_type=F32)
    mix = mix + jnp.dot(yrg_ref[...].astype(BF16), wo_ref[HY_W + HG_W:D_MODEL, :],
                        preferred_element_type=F32)
    x1 = x_ref[...] + g1_ref[...] * mix
    h = _rms(x1) * n2_ref[...]
    h = (h * (1.0 + sc2_ref[...]) + sh2_ref[...]).astype(BF16)
    step = D_FF // ff_chunks
    ff = None
    for c in range(ff_chunks):
        sl = slice(c * step, (c + 1) * step)
        gate = jnp.dot(h, wg_ref[:, sl], preferred_element_type=F32)
        up = jnp.dot(h, wu_ref[:, sl], preferred_element_type=F32)
        act = ((gate * _sigmoid(gate)) * up).astype(BF16)
        part = jnp.dot(act, wd_ref[sl, :], preferred_element_type=F32)
        ff = part if ff is None else ff + part
    x2 = x1 + g2_ref[...] * ff
    if final:
        x2 = _rms(x2) * fn_ref[...]
    o_ref[...] = x2


def _outffn(x2, y_hy, y_hg, y_rg, mod4, norm_w, w_out, wg, wu, wd, seq_len, final_w=None):
    t = x2.shape[0]
    tm = min(TOKEN_TILE, seq_len)
    tpb = seq_len // tm
    final = final_w is not None
    row = lambda i: (i, 0)
    const = lambda i: (0, 0)

    def resident(shape):
        return pl.BlockSpec(shape, const, pipeline_mode=pl.Buffered(1))

    in_specs = [pl.BlockSpec((tm, D_MODEL), row), pl.BlockSpec((tm, HY_W), row),
                pl.BlockSpec((tm, HG_W), row), pl.BlockSpec((tm, RG_W), row),
                _mod_spec(2, tpb), _mod_spec(3, tpb), _mod_spec(4, tpb), _mod_spec(5, tpb),
                pl.BlockSpec((1, D_MODEL), const),
                resident((D_MODEL, D_MODEL)), resident((D_MODEL, D_FF)), resident((D_MODEL, D_FF)),
                resident((D_FF, D_MODEL))]
    args = [x2, y_hy, y_hg, y_rg, mod4, mod4, mod4, mod4, norm_w.reshape(1, D_MODEL),
            w_out, wg, wu, wd]
    if final:
        in_specs.append(pl.BlockSpec((1, D_MODEL), const))
        args.append(final_w.reshape(1, D_MODEL))
    return pl.pallas_call(
        functools.partial(_outffn_kernel, final=final, ff_chunks=2),
        grid=(t // tm,),
        in_specs=in_specs,
        out_specs=pl.BlockSpec((tm, D_MODEL), row),
        out_shape=jax.ShapeDtypeStruct((t, D_MODEL), F32),
        compiler_params=_params(1),
        name="outffn",
    )(*args)


def _trunk(x, mods, lb_all, p, wb):
    b, seq_len, _ = x.shape
    x2 = x.reshape(b * seq_len, D_MODEL)
    tables = _dft_tables(seq_len)
    for l in range(DEPTH):
        mod4 = mods[l].reshape(b, 6, 1, D_MODEL)
        gain = p['out_norm_w'][l]
        v, x0, u_hg, u_rg = _inproj(x2, mod4, p['norm1_w'][l], wb['w_in'][l], seq_len,
                                    p['hy_conv_w'][l], p['hy_conv_b'][l])
        y_hy = _hyena(v.reshape(b, seq_len, HY_W), x0.reshape(b, seq_len, HY_W), p, l,
                      gain[0:HY_W], tables)
        y_hg = _hgrn2(u_hg.reshape(b, seq_len, HG_IN), lb_all[0, l], lb_all[1, l],
                      gain[HY_W:HY_W + HG_W])
        y_rg = _rglru(u_rg.reshape(b, seq_len, RG_IN), p, l, gain[HY_W + HG_W:])
        x2 = _outffn(x2, y_hy, y_hg, y_rg, mod4, p['norm2_w'][l], wb['w_out'][l], wb['ffn_wg'][l],
                     wb['ffn_wu'][l], wb['ffn_wd'][l], seq_len,
                     final_w=p['final_norm_w'] if l == DEPTH - 1 else None)
    return x2.reshape(b, seq_len, D_MODEL)


def kernel(x_prompt, x_sample, c_prompt, c_sample, w_in, w_out, out_norm_w, ada_w, ada_b, norm1_w, norm2_w, final_norm_w, hy_conv_w, hy_conv_b, hy_w1, hy_b1, hy_w2, hy_b2, hy_w3, hy_b3, hy_w4, hy_freq, hy_bias, hg_lb_logits, rg_conv_w, rg_conv_b, rg_wa, rg_ba, rg_wx, rg_bx, rg_lambda, ffn_wg, ffn_wu, ffn_wd):
    p = dict(out_norm_w=out_norm_w, norm1_w=norm1_w, norm2_w=norm2_w, final_norm_w=final_norm_w,
             hy_conv_w=hy_conv_w, hy_conv_b=hy_conv_b, hy_w1=hy_w1, hy_b1=hy_b1, hy_w2=hy_w2,
             hy_b2=hy_b2, hy_w3=hy_w3, hy_b3=hy_b3, hy_w4=hy_w4, hy_freq=hy_freq, hy_bias=hy_bias,
             rg_conv_w=rg_conv_w, rg_conv_b=rg_conv_b, rg_wa=rg_wa, rg_ba=rg_ba, rg_wx=rg_wx,
             rg_bx=rg_bx, rg_lambda=rg_lambda)
    wb = dict(w_in=w_in.astype(BF16), w_out=w_out.astype(BF16), ffn_wg=ffn_wg.astype(BF16),
              ffn_wu=ffn_wu.astype(BF16), ffn_wd=ffn_wd.astype(BF16))
    lb_all = jax.nn.softmax(hg_lb_logits.astype(F32), axis=1)
    lb_all = jnp.cumsum(lb_all, axis=1) - lb_all[:, :1]
    nb_p, nb_s = c_prompt.shape[0], c_sample.shape[0]
    rows = -(-(nb_p + nb_s) // SUBLANES) * SUBLANES
    c_all = jnp.pad(jnp.concatenate([c_prompt, c_sample]), ((0, rows - nb_p - nb_s), (0, 0)))
    mods = _ada(c_all, ada_w, ada_b)
    y_prompt = _trunk(x_prompt, mods[:, 0:nb_p], lb_all, p, wb)
    y_sample = _trunk(x_sample, mods[:, nb_p:nb_p + nb_s], lb_all, p, wb)
    return (y_prompt, y_sample)
```
